```python
import math
import jax, jax.numpy as jnp
from jax import lax
import numpy as np

D_MODEL = 2048
BATCH = 2
SEQ = 4096
DEPTH = 4
DEC_BATCH = 8
DEC_SEQ = 1
PAST_LEN = 16384
PAGE_SIZE = 128

N_A_LAYERS = DEPTH // 2
N_B_LAYERS = DEPTH - N_A_LAYERS
GROUP_CH = 16
N_GROUPS = D_MODEL // GROUP_CH
STATE_DIM = 64
QK_DIM = 64
V_DIM = 2 * QK_DIM
N_HEADS = D_MODEL // V_DIM
QK_WIDTH = N_HEADS * 2 * QK_DIM
V_WIDTH = N_HEADS * V_DIM
ROT_DIM = QK_DIM // 4
ROPE_THETA = 500000.0
D_FF = ((8 * D_MODEL // 3 + 127) // 128) * 128
Q_BLOCK = 128
NORM_EPS = 1e-6
DT_MIN = 0.001
DT_MAX = 0.1
NEG_INF = -1e30

kernel_name = "yoco_s5_diffattn_macaron_step"


def rmsnorm(x, g):
    x32 = x.astype(jnp.float32)
    y = x32 * lax.rsqrt(jnp.mean(x32 * x32, axis=-1, keepdims=True) + NORM_EPS) * g.astype(jnp.float32)
    return y.astype(x.dtype)


def swiglu(h, w_in, w_out):
    z = h @ w_in
    return (jax.nn.silu(z[..., :D_FF]) * z[..., D_FF:]) @ w_out


def rope_partial(x, pos):
    inv = ROPE_THETA ** (-jnp.arange(0, ROT_DIM, 2, dtype=jnp.float32) / ROT_DIM)
    ang = pos.astype(jnp.float32)[:, None] * inv[None, :]
    cos = jnp.cos(ang)[None, :, None, None, :]
    sin = jnp.sin(ang)[None, :, None, None, :]
    xr = x[..., :ROT_DIM].astype(jnp.float32)
    x1, x2 = xr[..., :ROT_DIM // 2], xr[..., ROT_DIM // 2:]
    rot = jnp.concatenate([x1 * cos - x2 * sin, x2 * cos + x1 * sin], axis=-1).astype(x.dtype)
    return jnp.concatenate([rot, x[..., ROT_DIM:]], axis=-1)


def s5_mixer(u, h0_re, h0_im, a_re, a_im, log_dt, b_re, b_im, c_re, c_im, d, w_glu):
    B, S, _ = u.shape
    u32 = u.astype(jnp.float32)
    ug = u32.reshape(B, S, N_GROUPS, GROUP_CH)
    ar = a_re.astype(jnp.float32)
    ai = a_im.astype(jnp.float32)
    dt = jnp.exp(log_dt.astype(jnp.float32))[:, None]
    mag = jnp.exp(dt * ar)
    ab_re = mag * jnp.cos(dt * ai)
    ab_im = mag * jnp.sin(dt * ai)
    den = ar * ar + ai * ai
    n_re = ab_re - 1.0
    n_im = ab_im
    f_re = (n_re * ar + n_im * ai) / den
    f_im = (n_im * ar - n_re * ai) / den
    br = b_re.astype(jnp.float32)
    bi = b_im.astype(jnp.float32)
    bb_re = f_re[..., None] * br - f_im[..., None] * bi
    bb_im = f_re[..., None] * bi + f_im[..., None] * br
    x_re = jnp.einsum('bsgc,gpc->bsgp', ug, bb_re)
    x_im = jnp.einsum('bsgc,gpc->bsgp', ug, bb_im)
    a_re_s = jnp.broadcast_to(ab_re[None, None], (1, S, N_GROUPS, STATE_DIM))
    a_im_s = jnp.broadcast_to(ab_im[None, None], (1, S, N_GROUPS, STATE_DIM))

    def combine(e1, e2):
        a1r, a1i, b1r, b1i = e1
        a2r, a2i, b2r, b2i = e2
        return (a2r * a1r - a2i * a1i,
                a2r * a1i + a2i * a1r,
                a2r * b1r - a2i * b1i + b2r,
                a2r * b1i + a2i * b1r + b2i)

    cr, ci, sr, si = lax.associative_scan(combine, (a_re_s, a_im_s, x_re, x_im), axis=1)
    h0r = h0_re.astype(jnp.float32)[:, None]
    h0i = h0_im.astype(jnp.float32)[:, None]
    hr = sr + cr * h0r - ci * h0i
    hi = si + cr * h0i + ci * h0r
    y = (jnp.einsum('bsgp,gcp->bsgc', hr, c_re.astype(jnp.float32))
         - jnp.einsum('bsgp,gcp->bsgc', hi, c_im.astype(jnp.float32)))
    y = y.reshape(B, S, D_MODEL) + d.astype(jnp.float32) * u32
    g = jax.nn.gelu(y).astype(u.dtype)
    z = g @ w_glu
    out = z[..., :D_MODEL] * jax.nn.sigmoid(z[..., D_MODEL:])
    return out, hr[:, -1], hi[:, -1]


def diff_attn_core(q, q_pos, k, v, k_pos, lam):
    s = jnp.einsum('bqhmd,bkhmd->bhmqk', q.astype(jnp.float32), k.astype(jnp.float32))
    mask = k_pos[None, :] <= q_pos[:, None]
    s = jnp.where(mask, s, NEG_INF)
    p = jax.nn.softmax(s, axis=-1)
    w = p[:, :, 0] - lam * p[:, :, 1]
    return jnp.einsum('bhqk,bkhe->bqhe', w, v.astype(jnp.float32))


def diff_attention(q, q_pos, k, v, k_pos, lam):
    B, Sq = q.shape[0], q.shape[1]
    if Sq <= Q_BLOCK:
        return diff_attn_core(q, q_pos, k, v, k_pos, lam)
    nb = Sq // Q_BLOCK
    qb = q.reshape(B, nb, Q_BLOCK, N_HEADS, 2, QK_DIM).swapaxes(0, 1)
    pb = q_pos.reshape(nb, Q_BLOCK)
    out = lax.map(lambda a: diff_attn_core(a[0], a[1], k, v, k_pos, lam), (qb, pb))
    return out.swapaxes(0, 1).reshape(B, Sq, N_HEADS, V_DIM)


def diff_attn_mixer(h, pos, k_all, v_all, k_pos, w_q, w_o, lam_params, subln_g, layer_idx):
    B, S, _ = h.shape
    q = (h @ w_q).reshape(B, S, N_HEADS, 2, QK_DIM)
    q = rope_partial(q, pos) * (QK_DIM ** -0.5)
    lam_init = 0.8 - 0.6 * math.exp(-0.3 * layer_idx)
    lp = lam_params.astype(jnp.float32)
    lam = jnp.exp(jnp.sum(lp[0] * lp[1])) - jnp.exp(jnp.sum(lp[2] * lp[3])) + lam_init
    o = diff_attention(q, pos, k_all, v_all, k_pos, lam)
    o = rmsnorm(o, subln_g) * (1.0 - lam_init)
    return o.reshape(B, S, V_WIDTH).astype(h.dtype) @ w_o


def trunk(x, pos, h0_re, h0_im, past_k, past_v,
          norm_g, ffn_w_in, ffn_w_out, s5_a_re, s5_a_im, s5_log_dt, s5_b_re, s5_b_im,
          s5_c_re, s5_c_im, s5_d, s5_w_glu, kv_norm_g, w_kv, attn_w_q, attn_w_o,
          diff_lambda, subln_g, final_norm_g):
    B, S, _ = x.shape
    ssm_re, ssm_im = [], []
    k_new = v_new = k_all = v_all = k_pos = None
    for l in range(DEPTH):
        if l == N_A_LAYERS:
            kv = rmsnorm(x, kv_norm_g) @ w_kv
            k_new = rope_partial(kv[..., :QK_WIDTH].reshape(B, S, N_HEADS, 2, QK_DIM), pos)
            v_new = kv[..., QK_WIDTH:].reshape(B, S, N_HEADS, V_DIM)
            if past_k is None:
                k_all, v_all, k_pos = k_new, v_new, pos
            else:
                k_all = jnp.concatenate([past_k.astype(k_new.dtype), k_new], axis=1)
                v_all = jnp.concatenate([past_v.astype(v_new.dtype), v_new], axis=1)
                k_pos = jnp.arange(k_all.shape[1])
        x = x + 0.5 * swiglu(rmsnorm(x, norm_g[l, 0]), ffn_w_in[l, 0], ffn_w_out[l, 0])
        h = rmsnorm(x, norm_g[l, 1])
        if l < N_A_LAYERS:
            y, hr, hi = s5_mixer(h, h0_re[l], h0_im[l], s5_a_re[l], s5_a_im[l], s5_log_dt[l],
                                 s5_b_re[l], s5_b_im[l], s5_c_re[l], s5_c_im[l], s5_d[l], s5_w_glu[l])
            ssm_re.append(hr)
            ssm_im.append(hi)
        else:
            j = l - N_A_LAYERS
            y = diff_attn_mixer(h, pos, k_all, v_all, k_pos, attn_w_q[j], attn_w_o[j],
                                diff_lambda[j], subln_g[j], l)
        x = x + y.astype(x.dtype)
        x = x + 0.5 * swiglu(rmsnorm(x, norm_g[l, 2]), ffn_w_in[l, 1], ffn_w_out[l, 1])
    return rmsnorm(x, final_norm_g), jnp.stack(ssm_re), jnp.stack(ssm_im), k_new, v_new


def setup_inputs(seed: int = 0) -> dict:
    key = jax.random.key(seed)
    ks = jax.random.split(key, 32)
    f32 = jnp.float32
    n_pages = PAST_LEN // PAGE_SIZE
    n_used = DEC_BATCH * n_pages
    n_phys = n_used + n_used // 4
    page_table = jax.random.permutation(ks[0], n_phys)[:n_used].reshape(DEC_BATCH, n_pages).astype(jnp.int32)
    nrm = lambda k, shape, s: jax.random.normal(k, shape, f32) * s
    return {
        "x_prompt": nrm(ks[1], (BATCH, SEQ, D_MODEL), 1.0),
        "x_sample": nrm(ks[2], (DEC_BATCH, DEC_SEQ, D_MODEL), 1.0),
        "state_ssm_re": nrm(ks[3], (N_A_LAYERS, DEC_BATCH, N_GROUPS, STATE_DIM), 0.5),
        "state_ssm_im": nrm(ks[4], (N_A_LAYERS, DEC_BATCH, N_GROUPS, STATE_DIM), 0.5),
        "cache_k": nrm(ks[5], (n_phys, PAGE_SIZE, N_HEADS, 2, QK_DIM), 1.0),
        "cache_v": nrm(ks[6], (n_phys, PAGE_SIZE, N_HEADS, V_DIM), 1.0),
        "page_table": page_table,
        "norm_g": 1.0 + nrm(ks[7], (DEPTH, 3, D_MODEL), 0.01),
        "ffn_w_in": nrm(ks[8], (DEPTH, 2, D_MODEL, 2 * D_FF), D_MODEL ** -0.5),
        "ffn_w_out": nrm(ks[9], (DEPTH, 2, D_FF, D_MODEL), D_FF ** -0.5),
        "s5_a_re": -0.5 + nrm(ks[10], (N_A_LAYERS, N_GROUPS, STATE_DIM), 0.01),
        "s5_a_im": math.pi * jnp.arange(STATE_DIM, dtype=f32) + nrm(ks[11], (N_A_LAYERS, N_GROUPS, STATE_DIM), 0.01),
        "s5_log_dt": jax.random.uniform(ks[12], (N_A_LAYERS, N_GROUPS), f32, math.log(DT_MIN), math.log(DT_MAX)),
        "s5_b_re": nrm(ks[13], (N_A_LAYERS, N_GROUPS, STATE_DIM, GROUP_CH), GROUP_CH ** -0.5),
        "s5_b_im": nrm(ks[14], (N_A_LAYERS, N_GROUPS, STATE_DIM, GROUP_CH), GROUP_CH ** -0.5),
        "s5_c_re": nrm(ks[15], (N_A_LAYERS, N_GROUPS, GROUP_CH, STATE_DIM), STATE_DIM ** -0.5),
        "s5_c_im": nrm(ks[16], (N_A_LAYERS, N_GROUPS, GROUP_CH, STATE_DIM), STATE_DIM ** -0.5),
        "s5_d": nrm(ks[17], (N_A_LAYERS, D_MODEL), 1.0),
        "s5_w_glu": nrm(ks[18], (N_A_LAYERS, D_MODEL, 2 * D_MODEL), D_MODEL ** -0.5),
        "kv_norm_g": 1.0 + nrm(ks[19], (D_MODEL,), 0.01),
        "w_kv": nrm(ks[20], (D_MODEL, QK_WIDTH + V_WIDTH), D_MODEL ** -0.5),
        "attn_w_q": nrm(ks[21], (N_B_LAYERS, D_MODEL, QK_WIDTH), D_MODEL ** -0.5),
        "attn_w_o": nrm(ks[22], (N_B_LAYERS, V_WIDTH, D_MODEL), V_WIDTH ** -0.5),
        "diff_lambda": nrm(ks[23], (N_B_LAYERS, 4, QK_DIM), 0.1),
        "subln_g": 1.0 + nrm(ks[24], (N_B_LAYERS, V_DIM), 0.01),
        "final_norm_g": 1.0 + nrm(ks[25], (D_MODEL,), 0.01),
    }


def reference(x_prompt, x_sample, state_ssm_re, state_ssm_im, cache_k, cache_v, page_table,
              norm_g, ffn_w_in, ffn_w_out, s5_a_re, s5_a_im, s5_log_dt, s5_b_re, s5_b_im,
              s5_c_re, s5_c_im, s5_d, s5_w_glu, kv_norm_g, w_kv, attn_w_q, attn_w_o,
              diff_lambda, subln_g, final_norm_g):
    weights = (norm_g, ffn_w_in, ffn_w_out, s5_a_re, s5_a_im, s5_log_dt, s5_b_re, s5_b_im,
               s5_c_re, s5_c_im, s5_d, s5_w_glu, kv_norm_g, w_kv, attn_w_q, attn_w_o,
               diff_lambda, subln_g, final_norm_g)
    b_p, s_p = x_prompt.shape[0], x_prompt.shape[1]
    h0 = jnp.zeros((N_A_LAYERS, b_p, N_GROUPS, STATE_DIM), jnp.float32)
    pos_p = jnp.arange(s_p)
    y_prompt, re_p, im_p, k_p, v_p = trunk(x_prompt, pos_p, h0, h0, None, None, *weights)
    b_s, s_s = x_sample.shape[0], x_sample.shape[1]
    past_len = page_table.shape[1] * cache_k.shape[1]
    past_k = cache_k[page_table].reshape(b_s, past_len, N_HEADS, 2, QK_DIM)
    past_v = cache_v[page_table].reshape(b_s, past_len, N_HEADS, V_DIM)
    pos_s = past_len + jnp.arange(s_s)
    y_sample, re_s, im_s, k_s, v_s = trunk(x_sample, pos_s, state_ssm_re, state_ssm_im,
                                          past_k, past_v, *weights)
    return (y_prompt, y_sample, re_p, im_p, k_p, v_p, re_s, im_s, k_s, v_s)
```

```python
import functools
import math

import jax
import jax.numpy as jnp
from jax import lax
from jax.experimental import pallas as pl
from jax.experimental.pallas import tpu as pltpu

F32 = jnp.float32
BF16 = jnp.bfloat16

GROUP_CH = 16
STATE_DIM = 64
QK_DIM = 64
V_DIM = 2 * QK_DIM
ROT_DIM = QK_DIM // 4
ROPE_THETA = 500000.0
NORM_EPS = 1e-6
NEG_INF = -1e30
Q_SCALE = QK_DIM ** -0.5

LANES = 128
SUBLANES = 8
OCT_GROUPS = LANES // GROUP_CH
OCT_STATE = OCT_GROUPS * STATE_DIM
VMEM_LIMIT = 56 * 1024 * 1024

ROW_TILE = 512
COL_TILE = 512
FF_TILE = 512
SAMPLE_ROWS = 16
PAGES_PER_STEP = 4


def _cparams(sem):
    return pltpu.CompilerParams(dimension_semantics=sem, vmem_limit_bytes=VMEM_LIMIT)


def _rms(x, g):
    return x * lax.rsqrt(jnp.mean(x * x, axis=-1, keepdims=True) + NORM_EPS) * g


def _sigmoid(x):
    return 1.0 / (1.0 + jnp.exp(-x))


def _ffn_kernel(x_ref, g_ref, wa_ref, wb_ref, wo_ref, gn_ref, *rest, n_ff, emit_x, emit_norm):
    outs, (h_scr, acc_scr) = rest[:-2], rest[-2:]
    j = pl.program_id(1)

    @pl.when(j == 0)
    def _():
        h_scr[...] = _rms(x_ref[...], g_ref[...]).astype(BF16)
        acc_scr[...] = jnp.zeros(acc_scr.shape, F32)

    h = h_scr[...]
    za = jnp.dot(h, wa_ref[...], preferred_element_type=F32)
    zb = jnp.dot(h, wb_ref[...], preferred_element_type=F32)
    act = (za * _sigmoid(za)) * zb
    acc_scr[...] += jnp.dot(act.astype(BF16), wo_ref[...], preferred_element_type=F32)

    @pl.when(j == n_ff - 1)
    def _():
        xo = x_ref[...] + 0.5 * acc_scr[...]
        k = 0
        if emit_x:
            outs[k][...] = xo
            k += 1
        if emit_norm:
            outs[k][...] = _rms(xo, gn_ref[...]).astype(outs[k].dtype)


def _ffn(x, g, w, gn, *, tm, emit_x=True, norm=None):
    m, d = x.shape
    wa, wb, wo = w
    n_ff, _, tf = wa.shape
    out_shape, out_specs = [], []
    if emit_x:
        out_shape.append(jax.ShapeDtypeStruct((m, d), F32))
        out_specs.append(pl.BlockSpec((tm, d), lambda i, j: (i, 0)))
    if norm is not None:
        out_shape.append(norm[0])
        out_specs.append(norm[1])
    res = pl.pallas_call(
        functools.partial(_ffn_kernel, n_ff=n_ff, emit_x=emit_x, emit_norm=norm is not None),
        grid=(m // tm, n_ff),
        in_specs=[
            pl.BlockSpec((tm, d), lambda i, j: (i, 0)),
            pl.BlockSpec((1, d), lambda i, j: (0, 0)),
            pl.BlockSpec((None, d, tf), lambda i, j: (j, 0, 0)),
            pl.BlockSpec((None, d, tf), lambda i, j: (j, 0, 0)),
            pl.BlockSpec((None, tf, d), lambda i, j: (j, 0, 0)),
            pl.BlockSpec((1, d), lambda i, j: (0, 0)),
        ],
        out_specs=out_specs,
        out_shape=out_shape,
        scratch_shapes=[pltpu.VMEM((tm, d), BF16), pltpu.VMEM((tm, d), F32)],
        compiler_params=_cparams(("parallel", "arbitrary")),
        name="ffn",
    )(x, g.reshape(1, d), wa, wb, wo, gn.reshape(1, d))
    return res


def _glu_kernel(a_ref, w1_ref, w2_ref, res_ref, o_ref):
    a = a_ref[...]
    z1 = jnp.dot(a, w1_ref[...], preferred_element_type=F32)
    z2 = jnp.dot(a, w2_ref[...], preferred_element_type=F32)
    o_ref[...] = res_ref[...] + z1 * _sigmoid(z2)


def _glu_proj(a, a_spec, w1, w2, res, *, tm, tn):
    m, n = res.shape
    kdim = w1.shape[0]
    return pl.pallas_call(
        _glu_kernel,
        grid=(m // tm, n // tn),
        in_specs=[
            a_spec,
            pl.BlockSpec((kdim, tn), lambda i, j: (0, j)),
            pl.BlockSpec((kdim, tn), lambda i, j: (0, j)),
            pl.BlockSpec((tm, tn), lambda i, j: (i, j)),
        ],
        out_specs=pl.BlockSpec((tm, tn), lambda i, j: (i, j)),
        out_shape=jax.ShapeDtypeStruct((m, n), F32),
        compiler_params=_cparams(("parallel", "arbitrary")),
        name="s5_glu",
    )(a, w1, w2, res)


def _res_proj_kernel(a_ref, w_ref, res_ref, o_ref):
    o_ref[...] = res_ref[...] + jnp.dot(a_ref[...], w_ref[...], preferred_element_type=F32)


def _res_proj(a, w, res, *, tm, tn):
    m, n = res.shape
    kdim = w.shape[0]
    return pl.pallas_call(
        _res_proj_kernel,
        grid=(m // tm, n // tn),
        in_specs=[
            pl.BlockSpec((tm, kdim), lambda i, j: (i, 0)),
            pl.BlockSpec((kdim, tn), lambda i, j: (0, j)),
            pl.BlockSpec((tm, tn), lambda i, j: (i, j)),
        ],
        out_specs=pl.BlockSpec((tm, tn), lambda i, j: (i, j)),
        out_shape=jax.ShapeDtypeStruct((m, n), F32),
        compiler_params=_cparams(("parallel", "arbitrary")),
        name="attn_out_proj",
    )(a, w, res)


def _rope_proj_kernel(a_ref, w_ref, cos_ref, sa_ref, sb_ref, *outs, rope, scale):
    z = jnp.dot(a_ref[...], w_ref[...], preferred_element_type=F32)
    if rope:
        cos, sa, sb = cos_ref[...], sa_ref[...], sb_ref[...]
        cols = []
        for c in range(z.shape[1] // LANES):
            zc = z[:, c * LANES:(c + 1) * LANES]
            up = pltpu.roll(zc, LANES - ROT_DIM // 2, 1)
            dn = pltpu.roll(zc, ROT_DIM // 2, 1)
            cols.append(zc * cos + up * sa + dn * sb)
        z = jnp.concatenate(cols, axis=1)
    if scale != 1.0:
        z = z * scale
    for o in outs:
        o[...] = z.astype(o.dtype)


def _rope_proj(a, w, tables, *, tm, tn, rope, scale=1.0, out_dtypes=(BF16,)):
    m, kdim = a.shape
    n = w.shape[1]
    n_tab = tables[0].shape[0] // tm
    tab_spec = pl.BlockSpec((tm, LANES), lambda i, j: (i % n_tab, 0))
    return pl.pallas_call(
        functools.partial(_rope_proj_kernel, rope=rope, scale=scale),
        grid=(m // tm, n // tn),
        in_specs=[
            pl.BlockSpec((tm, kdim), lambda i, j: (i, 0)),
            pl.BlockSpec((kdim, tn), lambda i, j: (0, j)),
            tab_spec, tab_spec, tab_spec,
        ],
        out_specs=[pl.BlockSpec((tm, tn), lambda i, j: (i, j)) for _ in out_dtypes],
        out_shape=[jax.ShapeDtypeStruct((m, n), dt) for dt in out_dtypes],
        compiler_params=_cparams(("parallel", "arbitrary")),
        name="rope_proj" if rope else "plain_proj",
    )(a, w, *tables)


def _rope_tables(pos):
    inv = ROPE_THETA ** (-jnp.arange(0, ROT_DIM, 2, dtype=F32) / ROT_DIM)
    ang = pos.astype(F32)[:, None] * inv[None, :]
    cos, sin = jnp.cos(ang), jnp.sin(ang)
    half = ROT_DIM // 2
    n = pos.shape[0]
    ones = jnp.ones((n, QK_DIM - ROT_DIM), F32)
    zeros = jnp.zeros((n, QK_DIM - half), F32)
    cos64 = jnp.concatenate([cos, cos, ones], axis=1)
    sa64 = jnp.concatenate([-sin, zeros], axis=1)
    sb64 = jnp.concatenate([jnp.zeros((n, half), F32), sin, jnp.zeros((n, QK_DIM - ROT_DIM), F32)], axis=1)
    rep = LANES // QK_DIM
    return tuple(jnp.tile(t, (1, rep)) for t in (cos64, sa64, sb64))


def _s5_step(ar, ai, hr, hi, xr, xi):
    return (ar * hr + xr) - ai * hi, (ar * hi + xi) + ai * hr


def _s5_scan_kernel(u_ref, h0_ref, bw_ref, cw_ref, a_ref, d_ref, g_ref, ht_ref,
                    x_scr, f_scr, hs_scr, *, seq, rc):
    n_t = seq // SUBLANES
    n_chunks = seq // rc
    p = OCT_STATE
    bw = bw_ref[...]

    def xproj(c, carry):
        r0 = pl.multiple_of(c * rc, rc)
        x_scr[pl.ds(r0, rc), :] = jnp.dot(u_ref[pl.ds(r0, rc), :].astype(BF16), bw,
                                          preferred_element_type=F32)
        return carry

    lax.fori_loop(0, n_chunks, xproj, 0)

    a = a_ref[...]
    ar = jnp.broadcast_to(a[0:1], (SUBLANES, p))
    ai = jnp.broadcast_to(a[1:2], (SUBLANES, p))

    def make_step(store):
        def step(t, carry):
            r0 = pl.multiple_of(t * SUBLANES, SUBLANES)
            nr, ni = _s5_step(ar, ai, carry[0], carry[1],
                              x_scr[pl.ds(r0, SUBLANES), 0:p], x_scr[pl.ds(r0, SUBLANES), p:2 * p])
            if store:
                x_scr[pl.ds(r0, SUBLANES), 0:p] = nr
                x_scr[pl.ds(r0, SUBLANES), p:2 * p] = ni
            return nr, ni
        return step

    zero = jnp.zeros((SUBLANES, p), F32)
    fr, fi = lax.fori_loop(0, n_t, make_step(False), (zero, zero), unroll=8)
    f_scr[:, 0:p] = fr
    f_scr[:, p:2 * p] = fi

    atr, ati = a[2:3], a[3:4]
    pr, pi = h0_ref[:, 0:p], h0_ref[:, p:2 * p]
    hs_scr[0:1, 0:p] = pr
    hs_scr[0:1, p:2 * p] = pi
    for s in range(1, SUBLANES):
        pr, pi = _s5_step(atr, ati, pr, pi, f_scr[s - 1:s, 0:p], f_scr[s - 1:s, p:2 * p])
        hs_scr[s:s + 1, 0:p] = pr
        hs_scr[s:s + 1, p:2 * p] = pi

    hr, hi = lax.fori_loop(0, n_t, make_step(True), (hs_scr[:, 0:p], hs_scr[:, p:2 * p]), unroll=8)
    ht_ref[:, 0:p] = hr[SUBLANES - 1:SUBLANES]
    ht_ref[:, p:2 * p] = hi[SUBLANES - 1:SUBLANES]

    cw = cw_ref[...]
    d = d_ref[...]

    def yproj(c, carry):
        r0 = pl.multiple_of(c * rc, rc)
        y = jnp.dot(x_scr[pl.ds(r0, rc), :].astype(BF16), cw, preferred_element_type=F32)
        y = y + d * u_ref[pl.ds(r0, rc), :]
        g_ref[pl.ds(r0, rc), :] = jax.nn.gelu(y).astype(BF16)
        return carry

    lax.fori_loop(0, n_chunks, yproj, 0)


def _s5_scan(u, h0, prm, *, rc):
    b, seq, d = u.shape
    n_oct = d // LANES
    p2 = 2 * OCT_STATE
    bw, cw, arows, drow = prm
    return pl.pallas_call(
        functools.partial(_s5_scan_kernel, seq=seq, rc=rc),
        grid=(b, n_oct),
        in_specs=[
            pl.BlockSpec((None, seq, LANES), lambda i, o: (i, 0, o)),
            pl.BlockSpec((None, None, 1, p2), lambda i, o: (i, o, 0, 0)),
            pl.BlockSpec((None, LANES, p2), lambda i, o: (o, 0, 0)),
            pl.BlockSpec((None, p2, LANES), lambda i, o: (o, 0, 0)),
            pl.BlockSpec((None, 4, OCT_STATE), lambda i, o: (o, 0, 0)),
            pl.BlockSpec((None, 1, LANES), lambda i, o: (o, 0, 0)),
        ],
        out_specs=[
            pl.BlockSpec((None, seq, LANES), lambda i, o: (i, 0, o)),
            pl.BlockSpec((None, None, 1, p2), lambda i, o: (i, o, 0, 0)),
        ],
        out_shape=[
            jax.ShapeDtypeStruct((b, seq, d), BF16),
            jax.ShapeDtypeStruct((b, n_oct, 1, p2), F32),
        ],
        scratch_shapes=[
            pltpu.VMEM((seq, p2), F32),
            pltpu.VMEM((SUBLANES, p2), F32),
            pltpu.VMEM((SUBLANES, p2), F32),
        ],
        compiler_params=_cparams(("parallel", "arbitrary")),
        name="s5_scan",
    )(u, h0, bw, cw, arows, drow)


def _s5_single_kernel(u_ref, h0_ref, bw_ref, cw_ref, a_ref, d_ref, g_ref, hn_ref):
    p = OCT_STATE
    u = u_ref[...]
    x = jnp.dot(u.astype(BF16), bw_ref[...], preferred_element_type=F32)
    a = a_ref[...]
    hr, hi = _s5_step(a[0:1], a[1:2], h0_ref[:, 0:p], h0_ref[:, p:2 * p], x[:, 0:p], x[:, p:2 * p])
    hn_ref[:, 0:p] = hr
    hn_ref[:, p:2 * p] = hi
    y = jnp.dot(hn_ref[...].astype(BF16), cw_ref[...], preferred_element_type=F32) + d_ref[...] * u
    g_ref[...] = jax.nn.gelu(y).astype(BF16)


def _s5_single(u, h0, prm):
    m, d = u.shape
    n_oct = d // LANES
    p2 = 2 * OCT_STATE
    bw, cw, arows, drow = prm
    return pl.pallas_call(
        _s5_single_kernel,
        grid=(n_oct,),
        in_specs=[
            pl.BlockSpec((m, LANES), lambda o: (0, o)),
            pl.BlockSpec((None, m, p2), lambda o: (o, 0, 0)),
            pl.BlockSpec((None, LANES, p2), lambda o: (o, 0, 0)),
            pl.BlockSpec((None, p2, LANES), lambda o: (o, 0, 0)),
            pl.BlockSpec((None, 4, OCT_STATE), lambda o: (o, 0, 0)),
            pl.BlockSpec((None, 1, LANES), lambda o: (o, 0, 0)),
        ],
        out_specs=[
            pl.BlockSpec((m, LANES), lambda o: (0, o)),
            pl.BlockSpec((None, m, p2), lambda o: (o, 0, 0)),
        ],
        out_shape=[
            jax.ShapeDtypeStruct((m, d), BF16),
            jax.ShapeDtypeStruct((n_oct, m, p2), F32),
        ],
        compiler_params=_cparams(("parallel",)),
        name="s5_single",
    )(u, h0, bw, cw, arows, drow)


def _s5_params(a_re, a_im, log_dt, b_re, b_im, c_re, c_im, d, n_pow):
    g = a_re.shape[0]
    n_oct = g // OCT_GROUPS
    ar, ai = a_re.astype(F32), a_im.astype(F32)
    dt = jnp.exp(log_dt.astype(F32))[:, None]
    mag = jnp.exp(dt * ar)
    ab_re = mag * jnp.cos(dt * ai)
    ab_im = mag * jnp.sin(dt * ai)
    den = ar * ar + ai * ai
    n_re = ab_re - 1.0
    n_im = ab_im
    f_re = (n_re * ar + n_im * ai) / den
    f_im = (n_im * ar - n_re * ai) / den
    br, bi = b_re.astype(F32), b_im.astype(F32)
    bb_re = f_re[..., None] * br - f_im[..., None] * bi
    bb_im = f_re[..., None] * bi + f_im[..., None] * br
    eye = jnp.eye(OCT_GROUPS, dtype=F32)

    def in_blocks(bb):
        t = bb.reshape(n_oct, OCT_GROUPS, STATE_DIM, GROUP_CH)
        t = jnp.einsum('ogpc,gh->ogchp', t, eye)
        return t.reshape(n_oct, LANES, OCT_STATE)

    def out_blocks(cc):
        t = cc.astype(F32).reshape(n_oct, OCT_GROUPS, GROUP_CH, STATE_DIM)
        t = jnp.einsum('ogcp,gh->ogphc', t, eye)
        return t.reshape(n_oct, OCT_STATE, LANES)

    bw = jnp.concatenate([in_blocks(bb_re), in_blocks(bb_im)], axis=2).astype(BF16)
    cw = jnp.concatenate([out_blocks(c_re), -out_blocks(c_im)], axis=1).astype(BF16)
    pw_re, pw_im = ab_re, ab_im
    for _ in range(n_pow):
        pw_re, pw_im = pw_re * pw_re - pw_im * pw_im, 2.0 * pw_re * pw_im
    arows = jnp.stack([t.reshape(n_oct, OCT_STATE) for t in (ab_re, ab_im, pw_re, pw_im)], axis=1)
    drow = d.astype(F32).reshape(n_oct, 1, LANES)
    return bw, cw, arows, drow


def _lambda(lp, lam_init):
    a = jnp.sum(lp[0:1] * lp[1:2], axis=1, keepdims=True)
    b = jnp.sum(lp[2:3] * lp[3:4], axis=1, keepdims=True)
    return jnp.exp(a) - jnp.exp(b) + lam_init


def _flash_kernel(qtab, ktab, q_ref, k_ref, v_ref, lp_ref, sg_ref, o_ref,
                  m1, l1, a1, m2, l2, a2, *, tq, lam_init):
    step = pl.program_id(2)
    qi, ki = qtab[step], ktab[step]

    @pl.when(ki == 0)
    def _():
        for m, l, a in ((m1, l1, a1), (m2, l2, a2)):
            m[...] = jnp.full(m.shape, NEG_INF, F32)
            l[...] = jnp.zeros(l.shape, F32)
            a[...] = jnp.zeros(a.shape, F32)

    q, k, v = q_ref[...], k_ref[...], v_ref[...]
    lane = lax.broadcasted_iota(jnp.int32, q.shape, 1)
    zero = jnp.zeros_like(q)
    row = lax.broadcasted_iota(jnp.int32, (tq, tq), 0) + qi * tq
    col = lax.broadcasted_iota(jnp.int32, (tq, tq), 1) + ki * tq
    mask = col <= row

    def update(qm, m_ref, l_ref, a_ref):
        s = lax.dot_general(qm, k, (((1,), (1,)), ((), ())), preferred_element_type=F32)
        s = jnp.where(mask, s, NEG_INF)
        m_old = m_ref[...]
        m_new = jnp.maximum(m_old, jnp.max(s, axis=1, keepdims=True))
        alpha = jnp.exp(m_old - m_new)
        p = jnp.exp(s - m_new)
        l_ref[...] = alpha * l_ref[...] + jnp.sum(p, axis=1, keepdims=True)
        a_ref[...] = alpha * a_ref[...] + jnp.dot(p.astype(BF16), v, preferred_element_type=F32)
        m_ref[...] = m_new

    update(jnp.where(lane < QK_DIM, q, zero), m1, l1, a1)
    update(jnp.where(lane >= QK_DIM, q, zero), m2, l2, a2)

    @pl.when(ki == qi)
    def _():
        lam = _lambda(lp_ref[...], lam_init)
        o = a1[...] / l1[...] - lam * (a2[...] / l2[...])
        o_ref[...] = (_rms(o, sg_ref[...]) * (1.0 - lam_init)).astype(o_ref.dtype)


def _flash(q, k, v, lp, sg, *, tq, lam_init):
    b, seq, width = q.shape
    n_heads = width // LANES
    nq = seq // tq
    pairs = [(i, j) for i in range(nq) for j in range(i + 1)]
    qtab = jnp.asarray([p[0] for p in pairs], jnp.int32)
    ktab = jnp.asarray([p[1] for p in pairs], jnp.int32)
    grid_spec = pltpu.PrefetchScalarGridSpec(
        num_scalar_prefetch=2,
        grid=(b, n_heads, len(pairs)),
        in_specs=[
            pl.BlockSpec((None, tq, LANES), lambda i, h, s, qt, kt: (i, qt[s], h)),
            pl.BlockSpec((None, tq, LANES), lambda i, h, s, qt, kt: (i, kt[s], h)),
            pl.BlockSpec((None, tq, LANES), lambda i, h, s, qt, kt: (i, kt[s], h)),
            pl.BlockSpec((4, QK_DIM), lambda i, h, s, qt, kt: (0, 0)),
            pl.BlockSpec((1, V_DIM), lambda i, h, s, qt, kt: (0, 0)),
        ],
        out_specs=pl.BlockSpec((None, tq, LANES), lambda i, h, s, qt, kt: (i, qt[s], h)),
        scratch_shapes=[
            pltpu.VMEM((tq, 1), F32), pltpu.VMEM((tq, 1), F32), pltpu.VMEM((tq, V_DIM), F32),
            pltpu.VMEM((tq, 1), F32), pltpu.VMEM((tq, 1), F32), pltpu.VMEM((tq, V_DIM), F32),
        ],
    )
    return pl.pallas_call(
        functools.partial(_flash_kernel, tq=tq, lam_init=lam_init),
        grid_spec=grid_spec,
        out_shape=jax.ShapeDtypeStruct((b, seq, width), BF16),
        compiler_params=_cparams(("parallel", "parallel", "arbitrary")),
        name="diff_flash",
    )(qtab, ktab, q, k, v, lp, sg.reshape(1, V_DIM))


def _decode_kernel(pt_ref, q_ref, *refs, pages, n_heads, lam_init):
    k_refs, v_refs = refs[:pages], refs[pages:2 * pages]
    knew_ref, vnew_ref, lp_ref, sg_ref, o_ref, m_scr, l_scr, acc_scr = refs[2 * pages:]
    j = pl.program_id(1)
    rows = 2 * n_heads
    width = n_heads * V_DIM

    @pl.when(j == 0)
    def _():
        m_scr[...] = jnp.full(m_scr.shape, NEG_INF, F32)
        l_scr[...] = jnp.zeros(l_scr.shape, F32)
        acc_scr[...] = jnp.zeros(acc_scr.shape, F32)

    rowi = lax.broadcasted_iota(jnp.int32, (rows, width), 0)
    lanei = lax.broadcasted_iota(jnp.int32, (rows, width), 1)
    head_bits = n_heads.bit_length() - 1
    head_sel = (lanei >> 7) == (rowi & (n_heads - 1))
    sel = head_sel & (((lanei >> 6) & 1) == (rowi >> head_bits))
    qf = jnp.where(sel, jnp.broadcast_to(q_ref[...].astype(F32), (rows, width)), 0.0)
    qrows = qf.astype(BF16)

    def online(s, pv_fn):
        m_old = m_scr[...]
        m_new = jnp.maximum(m_old, jnp.max(s, axis=1, keepdims=True))
        alpha = jnp.exp(m_old - m_new)
        p = jnp.exp(s - m_new)
        l_scr[...] = alpha * l_scr[...] + jnp.sum(p, axis=1, keepdims=True)
        acc_scr[...] = alpha * acc_scr[...] + pv_fn(p)
        m_scr[...] = m_new

    for i in range(pages):
        kb = k_refs[i][...].astype(BF16)
        s = lax.dot_general(qrows, kb, (((1,), (1,)), ((), ())), preferred_element_type=F32)
        vb = v_refs[i][...].astype(BF16)
        online(s, lambda p, vb=vb: jnp.dot(p.astype(BF16), vb, preferred_element_type=F32))

    @pl.when(j == pl.num_programs(1) - 1)
    def _():
        s_new = jnp.sum(qf * knew_ref[...].astype(F32), axis=1, keepdims=True)
        online(s_new, lambda p: p * vnew_ref[...])
        lam = _lambda(lp_ref[...], lam_init)
        on = acc_scr[...] / l_scr[...]
        o = on[0:n_heads] - lam * on[n_heads:rows]
        o = jnp.where(head_sel[0:n_heads], o, 0.0)
        ms = jnp.sum(o * o, axis=1, keepdims=True) * (1.0 / V_DIM)
        o = o * lax.rsqrt(ms + NORM_EPS) * sg_ref[...] * (1.0 - lam_init)
        o_ref[...] = jnp.sum(o, axis=0, keepdims=True).astype(o_ref.dtype)


def _decode_attn(q, k_new, v_new, cache_k, cache_v, page_table, lp, sg, *, lam_init):
    n_seq, n_pages = page_table.shape
    _, page, width = cache_k.shape
    n_heads = width // V_DIM
    assert n_heads & (n_heads - 1) == 0 and V_DIM == LANES and QK_DIM == LANES // 2
    pages = PAGES_PER_STEP if n_pages % PAGES_PER_STEP == 0 else 1
    row_spec = pl.BlockSpec((None, 1, width), lambda b, j, pt: (b, 0, 0))

    def page_spec(i):
        return pl.BlockSpec((None, page, width), lambda b, j, pt: (pt[b, j * pages + i], 0, 0))

    grid_spec = pltpu.PrefetchScalarGridSpec(
        num_scalar_prefetch=1,
        grid=(n_seq, n_pages // pages),
        in_specs=[row_spec] + [page_spec(i) for i in range(pages)] * 2 + [
            row_spec, row_spec,
            pl.BlockSpec((4, QK_DIM), lambda b, j, pt: (0, 0)),
            pl.BlockSpec((1, width), lambda b, j, pt: (0, 0)),
        ],
        out_specs=row_spec,
        scratch_shapes=[
            pltpu.VMEM((2 * n_heads, 1), F32), pltpu.VMEM((2 * n_heads, 1), F32),
            pltpu.VMEM((2 * n_heads, width), F32),
        ],
    )
    return pl.pallas_call(
        functools.partial(_decode_kernel, pages=pages, n_heads=n_heads, lam_init=lam_init),
        grid_spec=grid_spec,
        out_shape=jax.ShapeDtypeStruct((n_seq, 1, width), BF16),
        compiler_params=_cparams(("parallel", "arbitrary")),
        name="diff_decode",
    )(page_table, q, *([cache_k] * pages), *([cache_v] * pages), k_new, v_new, lp,
      jnp.tile(sg.reshape(1, V_DIM), (1, n_heads)))


def _prep_weights(norm_g, ffn_w_in, ffn_w_out, s5, s5_w_glu, w_kv, attn_w_q, attn_w_o, n_pow):
    depth, _, d, ff2 = ffn_w_in.shape
    d_ff = ff2 // 2
    n_ff = -(-d_ff // FF_TILE)
    pad = n_ff * FF_TILE - d_ff
    ffn = {}
    for l in range(depth):
        for k in range(2):
            w_in = ffn_w_in[l, k].astype(BF16)
            wa = jnp.pad(w_in[:, :d_ff], ((0, 0), (0, pad))).reshape(d, n_ff, FF_TILE).transpose(1, 0, 2)
            wb = jnp.pad(w_in[:, d_ff:], ((0, 0), (0, pad))).reshape(d, n_ff, FF_TILE).transpose(1, 0, 2)
            wo = jnp.pad(ffn_w_out[l, k].astype(BF16), ((0, pad), (0, 0))).reshape(n_ff, FF_TILE, d)
            ffn[l, k] = (wa, wb, wo)
    n_a = s5_w_glu.shape[0]
    s5_prm = [_s5_params(*(t[l] for t in s5), n_pow) for l in range(n_a)]
    glu = [(s5_w_glu[l, :, :d].astype(BF16), s5_w_glu[l, :, d:].astype(BF16)) for l in range(n_a)]
    qk_width = attn_w_q.shape[2]
    wk, wv = w_kv[:, :qk_width].astype(BF16), w_kv[:, qk_width:].astype(BF16)
    return ffn, s5_prm, glu, wk, wv, attn_w_q.astype(BF16), attn_w_o.astype(BF16)


def _trunk(x, prompt, weights, gains, h0, kv_cache, page_table, diff_lambda, subln_g):
    ffn_w, s5_prm, glu_w, wk, wv, wq, wo = weights
    norm_g, kv_norm_g, final_norm_g = gains
    depth = norm_g.shape[0]
    n_a = len(s5_prm)
    if prompt:
        b, seq, d = x.shape
        m = b * seq
        n_t = seq // SUBLANES
        tm = min(ROW_TILE, n_t)
        tq = min(ROW_TILE, seq)
        blocks_per_sub = n_t // tm
        pos = jnp.arange(seq)

        def perm_index(i, j):
            return (i // (SUBLANES * blocks_per_sub), i % blocks_per_sub,
                    (i // blocks_per_sub) % SUBLANES)

        perm_spec = pl.BlockSpec((None, tm, d), perm_index)
    else:
        m, d = x.shape
        tm = m
        pos = jnp.full((m,), page_table.shape[1] * kv_cache[0].shape[1], jnp.int32)
    tn = min(COL_TILE, d)
    tables = _rope_tables(pos)
    xf = x.reshape(m, d)
    plain = lambda dt: (jax.ShapeDtypeStruct((m, d), dt), pl.BlockSpec((tm, d), lambda i, j: (i, 0)))
    states, k_f32, v_f32, k_b, v_b, hn = [], None, None, None, None, None
    for l in range(depth):
        if l == n_a:
            k_f32, k_b = _rope_proj(hn, wk, tables, tm=tm, tn=tn, rope=True, out_dtypes=(F32, BF16))
            v_f32, v_b = _rope_proj(hn, wv, tables, tm=tm, tn=tn, rope=False, out_dtypes=(F32, BF16))
        if l < n_a and prompt:
            norm = (jax.ShapeDtypeStruct((b, n_t, SUBLANES * d), F32), perm_spec)
        else:
            norm = plain(F32 if l < n_a else BF16)
        xf, hn = _ffn(xf, norm_g[l, 0], ffn_w[l, 0], norm_g[l, 1], tm=tm, norm=norm)
        if l < n_a:
            if prompt:
                g, st = _s5_scan(hn.reshape(b, seq, d), h0[l], s5_prm[l], rc=min(ROW_TILE, seq))
                a, a_spec = g.reshape(b, n_t, SUBLANES * d), pl.BlockSpec((None, tm, d), perm_index)
            else:
                a, st = _s5_single(hn, h0[l], s5_prm[l])
                a_spec = pl.BlockSpec((tm, d), lambda i, j: (i, 0))
            states.append(st)
            xf = _glu_proj(a, a_spec, glu_w[l][0], glu_w[l][1], xf, tm=tm, tn=tn)
        else:
            jl = l - n_a
            lam_init = 0.8 - 0.6 * math.exp(-0.3 * l)
            q = _rope_proj(hn, wq[jl], tables, tm=tm, tn=tn, rope=True, scale=Q_SCALE)[0]
            if prompt:
                o = _flash(q.reshape(b, seq, -1), k_b.reshape(b, seq, -1), v_b.reshape(b, seq, -1),
                           diff_lambda[jl], subln_g[jl], tq=tq, lam_init=lam_init).reshape(m, -1)
            else:
                n_seq = page_table.shape[0]
                o = _decode_attn(q[:n_seq, None], k_b[:n_seq, None], v_f32[:n_seq, None],
                                 kv_cache[0], kv_cache[1], page_table,
                                 diff_lambda[jl], subln_g[jl], lam_init=lam_init)
                o = jnp.pad(o.reshape(n_seq, -1), ((0, m - n_seq), (0, 0)))
            xf = _res_proj(o, wo[jl], xf, tm=tm, tn=tn)
        if l == n_a - 1:
            xf, hn = _ffn(xf, norm_g[l, 2], ffn_w[l, 1], kv_norm_g, tm=tm, norm=plain(BF16))
        elif l == depth - 1:
            (y,) = _ffn(xf, norm_g[l, 2], ffn_w[l, 1], final_norm_g, tm=tm, emit_x=False, norm=plain(F32))
        else:
            (xf,) = _ffn(xf, norm_g[l, 2], ffn_w[l, 1], final_norm_g, tm=tm)
    return y, states, k_f32, v_f32


def kernel(x_prompt, x_sample, state_ssm_re, state_ssm_im, cache_k, cache_v, page_table, norm_g, ffn_w_in, ffn_w_out, s5_a_re, s5_a_im, s5_log_dt, s5_b_re, s5_b_im, s5_c_re, s5_c_im, s5_d, s5_w_glu, kv_norm_g, w_kv, attn_w_q, attn_w_o, diff_lambda, subln_g, final_norm_g):
    b, seq, d = x_prompt.shape
    n_seq = x_sample.shape[0]
    n_a, _, n_groups, p = state_ssm_re.shape
    n_oct = n_groups // OCT_GROUPS
    n_heads = cache_v.shape[2]
    n_t = seq // SUBLANES
    n_pow = n_t.bit_length() - 1
    assert seq == SUBLANES * (1 << n_pow) and x_sample.shape[1] == 1 and n_seq <= SAMPLE_ROWS

    s5 = (s5_a_re, s5_a_im, s5_log_dt, s5_b_re, s5_b_im, s5_c_re, s5_c_im, s5_d)
    weights = _prep_weights(norm_g, ffn_w_in, ffn_w_out, s5, s5_w_glu, w_kv, attn_w_q, attn_w_o, n_pow)
    gains = (norm_g, kv_norm_g, final_norm_g)

    h0_p = jnp.zeros((n_a, b, n_oct, 1, 2 * OCT_STATE), F32)
    y_p, st_p, k_p, v_p = _trunk(x_prompt, True, weights, gains, h0_p, None, None, diff_lambda, subln_g)

    def unblock_prompt(st, half):
        t = st[:, :, 0, half * OCT_STATE:(half + 1) * OCT_STATE]
        return t.reshape(b, n_groups, p)

    re_p = jnp.stack([unblock_prompt(s, 0) for s in st_p])
    im_p = jnp.stack([unblock_prompt(s, 1) for s in st_p])

    pad = SAMPLE_ROWS - n_seq
    xs = jnp.pad(x_sample.reshape(n_seq, d), ((0, pad), (0, 0)))

    def block_sample(t):
        t = t.reshape(n_a, n_seq, n_oct, OCT_STATE).transpose(0, 2, 1, 3)
        return jnp.pad(t, ((0, 0), (0, 0), (0, pad), (0, 0)))

    h0_s = jnp.concatenate([block_sample(state_ssm_re), block_sample(state_ssm_im)], axis=-1)
    kv_cache = (cache_k.reshape(cache_k.shape[0], cache_k.shape[1], -1),
                cache_v.reshape(cache_v.shape[0], cache_v.shape[1], -1))
    y_s, st_s, k_s, v_s = _trunk(xs, False, weights, gains, h0_s, kv_cache, page_table, diff_lambda, subln_g)

    def unblock_sample(st, half):
        t = st[:, :n_seq, half * OCT_STATE:(half + 1) * OCT_STATE]
        return t.transpose(1, 0, 2).reshape(n_seq, n_groups, p)

    re_s = jnp.stack([unblock_sample(s, 0) for s in st_s])
    im_s = jnp.stack([unblock_sample(s, 1) for s in st_s])

    return (y_p.reshape(b, seq, d), y_s[:n_seq].reshape(n_seq, 1, d), re_p, im_p,
            k_p.reshape(b, seq, n_heads, 2, QK_DIM), v_p.reshape(b, seq, n_heads, V_DIM),
            re_s, im_s,
            k_s[:n_seq].reshape(n_seq, 1, n_heads, 2, QK_DIM), v_s[:n_seq].reshape(n_seq, 1, n_heads, V_DIM))
```

```python
import functools
import math

import jax
import jax.numpy as jnp
from jax import lax
from jax.experimental import pallas as pl
from jax.experimental.pallas import tpu as pltpu

F32 = jnp.float32
BF16 = jnp.bfloat16

GROUP_CH = 16
STATE_DIM = 64
QK_DIM = 64
V_DIM = 2 * QK_DIM
ROT_DIM = QK_DIM // 4
ROPE_THETA = 500000.0
NORM_EPS = 1e-6
NEG_INF = -1e30
Q_SCALE = QK_DIM ** -0.5
LOG2_E = 1.4426950408889634

LANES = 128
SUBLANES = 8
OCT_GROUPS = LANES // GROUP_CH
OCT_STATE = OCT_GROUPS * STATE_DIM
VMEM_LIMIT = 56 * 1024 * 1024

ROW_TILE = 512
COL_TILE = 512
FF_TILE = 512
SAMPLE_ROWS = 16
PAGES_PER_STEP = 4


def _cparams(sem):
    return pltpu.CompilerParams(dimension_semantics=sem, vmem_limit_bytes=VMEM_LIMIT)


def _rms(x, g):
    return x * lax.rsqrt(jnp.mean(x * x, axis=-1, keepdims=True) + NORM_EPS) * g


def _sigmoid(x):
    return 1.0 / (1.0 + jnp.exp(-x))


def _ffn_kernel(x_ref, g_ref, gn_ref, wa_ref, wb_ref, wo_ref, *rest, n_main, has_tail, emit_x, emit_norm):
    tails, rest = (rest[:3], rest[3:]) if has_tail else ((), rest)
    outs, (h_scr, acc_scr) = rest[:-2], rest[-2:]
    j = pl.program_id(1)

    @pl.when(j == 0)
    def _():
        h_scr[...] = _rms(x_ref[...], g_ref[...]).astype(BF16)
        acc_scr[...] = jnp.zeros(acc_scr.shape, F32)

    def hidden(wa, wb, wo):
        h = h_scr[...]
        za = jnp.dot(h, wa[...], preferred_element_type=F32)
        zb = jnp.dot(h, wb[...], preferred_element_type=F32)
        act = (za * _sigmoid(za)) * zb
        return jnp.dot(act.astype(BF16), wo[...], preferred_element_type=F32)

    acc_scr[...] += hidden(wa_ref, wb_ref, wo_ref)

    @pl.when(j == n_main - 1)
    def _():
        acc = acc_scr[...]
        if has_tail:
            acc = acc + hidden(*tails)
        xo = x_ref[...] + 0.5 * acc
        k = 0
        if emit_x:
            outs[k][...] = xo
            k += 1
        if emit_norm:
            outs[k][...] = _rms(xo, gn_ref[...]).astype(outs[k].dtype)


def _ffn(x, g, w, lk, gn, *, tm, emit_x=True, norm=None):
    m, d = x.shape
    l, k = lk
    wa, wb, wo, tails = w
    n_main = wa.shape[3] // FF_TILE
    out_shape, out_specs = [], []
    if emit_x:
        out_shape.append(jax.ShapeDtypeStruct((m, d), F32))
        out_specs.append(pl.BlockSpec((tm, d), lambda i, j: (i, 0)))
    if norm is not None:
        out_shape.append(norm[0])
        out_specs.append(norm[1])
    row = pl.BlockSpec((1, d), lambda i, j: (0, 0))
    in_specs = [
        pl.BlockSpec((tm, d), lambda i, j: (i, 0)), row, row,
        pl.BlockSpec((None, None, d, FF_TILE), lambda i, j: (l, k, 0, j)),
        pl.BlockSpec((None, None, d, FF_TILE), lambda i, j: (l, k, 0, j)),
        pl.BlockSpec((None, None, FF_TILE, d), lambda i, j: (l, k, j, 0)),
    ]
    for t in tails:
        in_specs.append(pl.BlockSpec((None, None) + t.shape[2:], lambda i, j: (l, k, 0, 0)))
    return pl.pallas_call(
        functools.partial(_ffn_kernel, n_main=n_main, has_tail=bool(tails), emit_x=emit_x,
                          emit_norm=norm is not None),
        grid=(m // tm, n_main),
        in_specs=in_specs,
        out_specs=out_specs,
        out_shape=out_shape,
        scratch_shapes=[pltpu.VMEM((tm, d), BF16), pltpu.VMEM((tm, d), F32)],
        compiler_params=_cparams(("parallel", "arbitrary")),
        name="ffn",
    )(x, g.reshape(1, d), gn.reshape(1, d), wa, wb, wo, *tails)


def _glu_kernel(a_ref, w1_ref, w2_ref, res_ref, o_ref):
    a = a_ref[...]
    z1 = jnp.dot(a, w1_ref[...], preferred_element_type=F32)
    z2 = jnp.dot(a, w2_ref[...], preferred_element_type=F32)
    o_ref[...] = res_ref[...] + z1 * _sigmoid(z2)


def _glu_proj(a, a_spec, w, l, res, *, tm, tn):
    m, n = res.shape
    kdim = w.shape[1]
    nb = n // tn
    return pl.pallas_call(
        _glu_kernel,
        grid=(m // tm, nb),
        in_specs=[
            a_spec,
            pl.BlockSpec((None, kdim, tn), lambda i, j: (l, 0, j)),
            pl.BlockSpec((None, kdim, tn), lambda i, j: (l, 0, nb + j)),
            pl.BlockSpec((tm, tn), lambda i, j: (i, j)),
        ],
        out_specs=pl.BlockSpec((tm, tn), lambda i, j: (i, j)),
        out_shape=jax.ShapeDtypeStruct((m, n), F32),
        compiler_params=_cparams(("parallel", "arbitrary")),
        name="s5_glu",
    )(a, w, w, res)


def _res_proj_kernel(a_ref, w_ref, res_ref, o_ref):
    o_ref[...] = res_ref[...] + jnp.dot(a_ref[...], w_ref[...], preferred_element_type=F32)


def _res_proj(a, w, l, res, *, tm, tn):
    m, n = res.shape
    kdim = w.shape[1]
    return pl.pallas_call(
        _res_proj_kernel,
        grid=(m // tm, n // tn),
        in_specs=[
            pl.BlockSpec((tm, kdim), lambda i, j: (i, 0)),
            pl.BlockSpec((None, kdim, tn), lambda i, j: (l, 0, j)),
            pl.BlockSpec((tm, tn), lambda i, j: (i, j)),
        ],
        out_specs=pl.BlockSpec((tm, tn), lambda i, j: (i, j)),
        out_shape=jax.ShapeDtypeStruct((m, n), F32),
        compiler_params=_cparams(("parallel", "arbitrary")),
        name="attn_out_proj",
    )(a, w, res)


def _rope_proj_kernel(a_ref, w_ref, cos_ref, sa_ref, sb_ref, *outs, rope, scale, transposed):
    z = jnp.dot(a_ref[...], w_ref[...], preferred_element_type=F32)
    if rope:
        cos, sa, sb = cos_ref[...], sa_ref[...], sb_ref[...]
        cols = []
        for c in range(z.shape[1] // LANES):
            zc = z[:, c * LANES:(c + 1) * LANES]
            up = pltpu.roll(zc, LANES - ROT_DIM // 2, 1)
            dn = pltpu.roll(zc, ROT_DIM // 2, 1)
            cols.append(zc * cos + up * sa + dn * sb)
        z = jnp.concatenate(cols, axis=1)
    if scale != 1.0:
        z = z * scale
    zt = z.T if any(transposed) else None
    for o, t in zip(outs, transposed):
        o[...] = (zt if t else z).astype(o.dtype)


def _rope_proj(a, w, w_index, n, tables, *, tm, tn, rope, scale=1.0, outs=((BF16, False),), batch=1):
    m, kdim = a.shape
    n_tab = tables[0].shape[0] // tm
    tab_spec = pl.BlockSpec((tm, LANES), lambda i, j: (i % n_tab, 0))
    bpb = m // batch // tm
    out_specs, out_shape = [], []
    for dt, t in outs:
        if t:
            out_shape.append(jax.ShapeDtypeStruct((batch, n, m // batch), dt))
            out_specs.append(pl.BlockSpec((None, tn, tm), lambda i, j: (i // bpb, j, i % bpb)))
        else:
            out_shape.append(jax.ShapeDtypeStruct((m, n), dt))
            out_specs.append(pl.BlockSpec((tm, tn), lambda i, j: (i, j)))
    w_block = (None,) * (w.ndim - 2) + (kdim, tn)
    return pl.pallas_call(
        functools.partial(_rope_proj_kernel, rope=rope, scale=scale, transposed=tuple(t for _, t in outs)),
        grid=(m // tm, n // tn),
        in_specs=[
            pl.BlockSpec((tm, kdim), lambda i, j: (i, 0)),
            pl.BlockSpec(w_block, lambda i, j: w_index(j)),
            tab_spec, tab_spec, tab_spec,
        ],
        out_specs=out_specs,
        out_shape=out_shape,
        compiler_params=_cparams(("parallel", "arbitrary")),
        name="rope_proj" if rope else "plain_proj",
    )(a, w, *tables)


def _rope_tables(pos):
    inv = ROPE_THETA ** (-jnp.arange(0, ROT_DIM, 2, dtype=F32) / ROT_DIM)
    ang = pos.astype(F32)[:, None] * inv[None, :]
    cos, sin = jnp.cos(ang), jnp.sin(ang)
    half = ROT_DIM // 2
    n = pos.shape[0]
    ones = jnp.ones((n, QK_DIM - ROT_DIM), F32)
    zeros = jnp.zeros((n, QK_DIM - half), F32)
    cos64 = jnp.concatenate([cos, cos, ones], axis=1)
    sa64 = jnp.concatenate([-sin, zeros], axis=1)
    sb64 = jnp.concatenate([jnp.zeros((n, half), F32), sin, jnp.zeros((n, QK_DIM - ROT_DIM), F32)], axis=1)
    rep = LANES // QK_DIM
    return tuple(jnp.tile(t, (1, rep)) for t in (cos64, sa64, sb64))


def _s5_step(ar, ai, hr, hi, xr, xi):
    return (ar * hr + xr) - ai * hi, (ar * hi + xi) + ai * hr


def _s5_scan_kernel(u_ref, h0_ref, bw_ref, cw_ref, a_ref, d_ref, g_ref, ht_ref,
                    x_scr, f_scr, hs_scr, *, seq, rc):
    n_t = seq // SUBLANES
    n_chunks = seq // rc
    p = OCT_STATE
    bw = bw_ref[...]

    def xproj(c, carry):
        r0 = pl.multiple_of(c * rc, rc)
        x_scr[pl.ds(r0, rc), :] = jnp.dot(u_ref[pl.ds(r0, rc), :].astype(BF16), bw,
                                          preferred_element_type=F32)
        return carry

    lax.fori_loop(0, n_chunks, xproj, 0)

    a = a_ref[...]
    ar = jnp.broadcast_to(a[0:1], (SUBLANES, p))
    ai = jnp.broadcast_to(a[1:2], (SUBLANES, p))

    def make_step(store):
        def step(t, carry):
            r0 = pl.multiple_of(t * SUBLANES, SUBLANES)
            nr, ni = _s5_step(ar, ai, carry[0], carry[1],
                              x_scr[pl.ds(r0, SUBLANES), 0:p], x_scr[pl.ds(r0, SUBLANES), p:2 * p])
            if store:
                x_scr[pl.ds(r0, SUBLANES), 0:p] = nr
                x_scr[pl.ds(r0, SUBLANES), p:2 * p] = ni
            return nr, ni
        return step

    zero = jnp.zeros((SUBLANES, p), F32)
    fr, fi = lax.fori_loop(0, n_t, make_step(False), (zero, zero), unroll=8)
    f_scr[:, 0:p] = fr
    f_scr[:, p:2 * p] = fi

    atr, ati = a[2:3], a[3:4]
    pr, pi = h0_ref[:, 0:p], h0_ref[:, p:2 * p]
    hs_scr[0:1, 0:p] = pr
    hs_scr[0:1, p:2 * p] = pi
    for s in range(1, SUBLANES):
        pr, pi = _s5_step(atr, ati, pr, pi, f_scr[s - 1:s, 0:p], f_scr[s - 1:s, p:2 * p])
        hs_scr[s:s + 1, 0:p] = pr
        hs_scr[s:s + 1, p:2 * p] = pi

    hr, hi = lax.fori_loop(0, n_t, make_step(True), (hs_scr[:, 0:p], hs_scr[:, p:2 * p]), unroll=8)
    ht_ref[:, 0:p] = hr[SUBLANES - 1:SUBLANES]
    ht_ref[:, p:2 * p] = hi[SUBLANES - 1:SUBLANES]

    cw = cw_ref[...]
    d = d_ref[...]

    def yproj(c, carry):
        r0 = pl.multiple_of(c * rc, rc)
        y = jnp.dot(x_scr[pl.ds(r0, rc), :].astype(BF16), cw, preferred_element_type=F32)
        y = y + d * u_ref[pl.ds(r0, rc), :]
        g_ref[pl.ds(r0, rc), :] = jax.nn.gelu(y).astype(BF16)
        return carry

    lax.fori_loop(0, n_chunks, yproj, 0)


def _s5_scan(u, h0, prm, *, rc):
    b, seq, d = u.shape
    n_oct = d // LANES
    p2 = 2 * OCT_STATE
    bw, cw, arows, drow = prm
    return pl.pallas_call(
        functools.partial(_s5_scan_kernel, seq=seq, rc=rc),
        grid=(b, n_oct),
        in_specs=[
            pl.BlockSpec((None, seq, LANES), lambda i, o: (i, 0, o)),
            pl.BlockSpec((None, None, 1, p2), lambda i, o: (i, o, 0, 0)),
            pl.BlockSpec((None, LANES, p2), lambda i, o: (o, 0, 0)),
            pl.BlockSpec((None, p2, LANES), lambda i, o: (o, 0, 0)),
            pl.BlockSpec((None, 4, OCT_STATE), lambda i, o: (o, 0, 0)),
            pl.BlockSpec((None, 1, LANES), lambda i, o: (o, 0, 0)),
        ],
        out_specs=[
            pl.BlockSpec((None, seq, LANES), lambda i, o: (i, 0, o)),
            pl.BlockSpec((None, None, 1, p2), lambda i, o: (i, o, 0, 0)),
        ],
        out_shape=[
            jax.ShapeDtypeStruct((b, seq, d), BF16),
            jax.ShapeDtypeStruct((b, n_oct, 1, p2), F32),
        ],
        scratch_shapes=[
            pltpu.VMEM((seq, p2), F32),
            pltpu.VMEM((SUBLANES, p2), F32),
            pltpu.VMEM((SUBLANES, p2), F32),
        ],
        compiler_params=_cparams(("parallel", "arbitrary")),
        name="s5_scan",
    )(u, h0, bw, cw, arows, drow)


def _s5_single_kernel(u_ref, h0_ref, bw_ref, cw_ref, a_ref, d_ref, g_ref, hn_ref):
    p = OCT_STATE
    u = u_ref[...]
    x = jnp.dot(u.astype(BF16), bw_ref[...], preferred_element_type=F32)
    a = a_ref[...]
    hr, hi = _s5_step(a[0:1], a[1:2], h0_ref[:, 0:p], h0_ref[:, p:2 * p], x[:, 0:p], x[:, p:2 * p])
    hn_ref[:, 0:p] = hr
    hn_ref[:, p:2 * p] = hi
    y = jnp.dot(hn_ref[...].astype(BF16), cw_ref[...], preferred_element_type=F32) + d_ref[...] * u
    g_ref[...] = jax.nn.gelu(y).astype(BF16)


def _s5_single(u, h0, prm):
    m, d = u.shape
    n_oct = d // LANES
    p2 = 2 * OCT_STATE
    bw, cw, arows, drow = prm
    return pl.pallas_call(
        _s5_single_kernel,
        grid=(n_oct,),
        in_specs=[
            pl.BlockSpec((m, LANES), lambda o: (0, o)),
            pl.BlockSpec((None, m, p2), lambda o: (o, 0, 0)),
            pl.BlockSpec((None, LANES, p2), lambda o: (o, 0, 0)),
            pl.BlockSpec((None, p2, LANES), lambda o: (o, 0, 0)),
            pl.BlockSpec((None, 4, OCT_STATE), lambda o: (o, 0, 0)),
            pl.BlockSpec((None, 1, LANES), lambda o: (o, 0, 0)),
        ],
        out_specs=[
            pl.BlockSpec((m, LANES), lambda o: (0, o)),
            pl.BlockSpec((None, m, p2), lambda o: (o, 0, 0)),
        ],
        out_shape=[
            jax.ShapeDtypeStruct((m, d), BF16),
            jax.ShapeDtypeStruct((n_oct, m, p2), F32),
        ],
        compiler_params=_cparams(("parallel",)),
        name="s5_single",
    )(u, h0, bw, cw, arows, drow)


def _s5_params(a_re, a_im, log_dt, b_re, b_im, c_re, c_im, d, n_pow):
    g = a_re.shape[0]
    n_oct = g // OCT_GROUPS
    ar, ai = a_re.astype(F32), a_im.astype(F32)
    dt = jnp.exp(log_dt.astype(F32))[:, None]
    mag = jnp.exp(dt * ar)
    ab_re = mag * jnp.cos(dt * ai)
    ab_im = mag * jnp.sin(dt * ai)
    den = ar * ar + ai * ai
    n_re = ab_re - 1.0
    n_im = ab_im
    f_re = (n_re * ar + n_im * ai) / den
    f_im = (n_im * ar - n_re * ai) / den
    br, bi = b_re.astype(F32), b_im.astype(F32)
    bb_re = f_re[..., None] * br - f_im[..., None] * bi
    bb_im = f_re[..., None] * bi + f_im[..., None] * br
    eye = jnp.eye(OCT_GROUPS, dtype=F32)

    def in_blocks(bb):
        t = bb.reshape(n_oct, OCT_GROUPS, STATE_DIM, GROUP_CH)
        t = jnp.einsum('ogpc,gh->ogchp', t, eye)
        return t.reshape(n_oct, LANES, OCT_STATE)

    def out_blocks(cc):
        t = cc.astype(F32).reshape(n_oct, OCT_GROUPS, GROUP_CH, STATE_DIM)
        t = jnp.einsum('ogcp,gh->ogphc', t, eye)
        return t.reshape(n_oct, OCT_STATE, LANES)

    bw = jnp.concatenate([in_blocks(bb_re), in_blocks(bb_im)], axis=2).astype(BF16)
    cw = jnp.concatenate([out_blocks(c_re), -out_blocks(c_im)], axis=1).astype(BF16)
    pw_re, pw_im = ab_re, ab_im
    for _ in range(n_pow):
        pw_re, pw_im = pw_re * pw_re - pw_im * pw_im, 2.0 * pw_re * pw_im
    arows = jnp.stack([t.reshape(n_oct, OCT_STATE) for t in (ab_re, ab_im, pw_re, pw_im)], axis=1)
    drow = d.astype(F32).reshape(n_oct, 1, LANES)
    return bw, cw, arows, drow


def _lambda(lp, lam_init):
    a = jnp.sum(lp[0:1] * lp[1:2], axis=1, keepdims=True)
    b = jnp.sum(lp[2:3] * lp[3:4], axis=1, keepdims=True)
    return jnp.exp(a) - jnp.exp(b) + lam_init


def _flash_kernel(qtab, ktab, qt_ref, k_ref, vt_ref, lp_ref, sg_ref, o_ref,
                  m1, l1, a1, m2, l2, a2, *, lam_init):
    step = pl.program_id(2)
    qi, ki = qtab[step], ktab[step]

    @pl.when(ki == 0)
    def _():
        for m, l, a in ((m1, l1, a1), (m2, l2, a2)):
            m[...] = jnp.full(m.shape, NEG_INF, F32)
            l[...] = jnp.zeros(l.shape, F32)
            a[...] = jnp.zeros(a.shape, F32)

    def block(diagonal):
        qt, k, vt = qt_ref[...], k_ref[...], vt_ref[...]
        feat = lax.broadcasted_iota(jnp.int32, qt.shape, 0)
        zero = jnp.zeros_like(qt)
        tk, tq = k.shape[0], qt.shape[1]
        if diagonal:
            keep = (lax.broadcasted_iota(jnp.int32, (tk, tq), 0)
                    <= lax.broadcasted_iota(jnp.int32, (tk, tq), 1))
        for qm, m_ref, l_ref, a_ref in ((jnp.where(feat < QK_DIM, qt, zero), m1, l1, a1),
                                        (jnp.where(feat >= QK_DIM, qt, zero), m2, l2, a2)):
            st = jnp.dot(k, qm, preferred_element_type=F32)
            if diagonal:
                st = jnp.where(keep, st, NEG_INF)
            m_old = m_ref[...]
            m_new = jnp.maximum(m_old, jnp.max(st, axis=0, keepdims=True))
            alpha = jnp.exp2(m_old - m_new)
            pt = jnp.exp2(st - m_new)
            l_ref[...] = alpha * l_ref[...] + jnp.sum(pt, axis=0, keepdims=True)
            a_ref[...] = alpha * a_ref[...] + jnp.dot(vt, pt.astype(BF16), preferred_element_type=F32)
            m_ref[...] = m_new

    @pl.when(ki < qi)
    def _():
        block(False)

    @pl.when(ki == qi)
    def _():
        block(True)
        lam = _lambda(lp_ref[...], lam_init)
        ot = a1[...] / l1[...] - lam * (a2[...] / l2[...])
        ms = jnp.mean(ot * ot, axis=0, keepdims=True)
        ot = ot * lax.rsqrt(ms + NORM_EPS) * sg_ref[...] * (1.0 - lam_init)
        o_ref[...] = ot.T.astype(o_ref.dtype)


def _flash(qt, k, vt, lp, sg, *, tq, lam_init):
    b, seq, width = k.shape
    n_heads = width // LANES
    nq = seq // tq
    pairs = [(i, j) for i in range(nq) for j in range(i + 1)]
    qtab = jnp.asarray([p[0] for p in pairs], jnp.int32)
    ktab = jnp.asarray([p[1] for p in pairs], jnp.int32)
    grid_spec = pltpu.PrefetchScalarGridSpec(
        num_scalar_prefetch=2,
        grid=(b, n_heads, len(pairs)),
        in_specs=[
            pl.BlockSpec((None, LANES, tq), lambda i, h, s, qt_, kt_: (i, h, qt_[s])),
            pl.BlockSpec((None, tq, LANES), lambda i, h, s, qt_, kt_: (i, kt_[s], h)),
            pl.BlockSpec((None, LANES, tq), lambda i, h, s, qt_, kt_: (i, h, kt_[s])),
            pl.BlockSpec((4, QK_DIM), lambda i, h, s, qt_, kt_: (0, 0)),
            pl.BlockSpec((V_DIM, 1), lambda i, h, s, qt_, kt_: (0, 0)),
        ],
        out_specs=pl.BlockSpec((None, tq, LANES), lambda i, h, s, qt_, kt_: (i, qt_[s], h)),
        scratch_shapes=[
            pltpu.VMEM((1, tq), F32), pltpu.VMEM((1, tq), F32), pltpu.VMEM((V_DIM, tq), F32),
            pltpu.VMEM((1, tq), F32), pltpu.VMEM((1, tq), F32), pltpu.VMEM((V_DIM, tq), F32),
        ],
    )
    return pl.pallas_call(
        functools.partial(_flash_kernel, lam_init=lam_init),
        grid_spec=grid_spec,
        out_shape=jax.ShapeDtypeStruct((b, seq, width), BF16),
        compiler_params=_cparams(("parallel", "parallel", "arbitrary")),
        name="diff_flash",
    )(qtab, ktab, qt, k, vt, lp, sg.reshape(V_DIM, 1))


def _decode_kernel(pt_ref, qcol_ref, *refs, pages, n_heads, lam_init):
    k_refs, v_refs = refs[:pages], refs[pages:2 * pages]
    kn_ref, vn_ref, rep_ref, lp_ref, sg_ref, o_ref, s_scr, m_scr, l_scr, acc_scr = refs[2 * pages:]
    j = pl.program_id(1)
    rows = 2 * n_heads
    n_tok = kn_ref.shape[1]

    @pl.when(j == 0)
    def _():
        m_scr[...] = jnp.full(m_scr.shape, NEG_INF, F32)
        l_scr[...] = jnp.zeros(l_scr.shape, F32)
        acc_scr[...] = jnp.zeros(acc_scr.shape, F32)

    rowi = lax.broadcasted_iota(jnp.int32, (rows, n_tok * n_heads), 0)
    coli = lax.broadcasted_iota(jnp.int32, (rows, n_tok * n_heads), 1)
    head_sel = (coli & (n_heads - 1)) == (rowi >> 1)

    def page(kt_ref, v_ref, n_valid):
        prod = kt_ref[...] * qcol_ref[...]
        for c in range(rows):
            s_scr[c:c + 1, :] = jnp.sum(prod[c * QK_DIM:(c + 1) * QK_DIM, :], axis=0, keepdims=True)
        s = s_scr[...]
        if n_valid is not None:
            s = jnp.where(lax.broadcasted_iota(jnp.int32, s.shape, 1) < n_valid, s, NEG_INF)
        m_old = m_scr[...]
        m_new = jnp.maximum(m_old, jnp.max(s, axis=1, keepdims=True))
        alpha = jnp.exp2(m_old - m_new)
        p = jnp.exp2(s - m_new)
        l_scr[...] = alpha * l_scr[...] + jnp.sum(p, axis=1, keepdims=True)
        spread = jnp.dot(p.astype(BF16), rep_ref[...], preferred_element_type=F32)
        spread = jnp.where(head_sel, spread, 0.0).astype(BF16)
        pv = jnp.dot(spread, v_ref[...].astype(BF16), preferred_element_type=F32)
        acc_scr[...] = alpha * acc_scr[...] + pv
        m_scr[...] = m_new

    for i in range(pages):
        page(k_refs[i], v_refs[i], None)

    @pl.when(j == pl.num_programs(1) - 1)
    def _():
        page(kn_ref, vn_ref, 1)
        lam = _lambda(lp_ref[...], lam_init)
        s_scr[...] = acc_scr[...] / l_scr[...]
        o = s_scr[pl.ds(0, n_heads, stride=2), :] - lam * s_scr[pl.ds(1, n_heads, stride=2), :]
        o_ref[...] = (_rms(o, sg_ref[...]) * (1.0 - lam_init)).astype(o_ref.dtype)


def _decode_attn(qcol, kn, vn, cache_kt, cache_v2, page_table, lp, sg, *, lam_init):
    n_seq, n_pages = page_table.shape
    _, width, page = cache_kt.shape
    n_heads = width // V_DIM
    assert n_heads & (n_heads - 1) == 0 and page == LANES
    pages = PAGES_PER_STEP if n_pages % PAGES_PER_STEP == 0 else 1
    seq_k = pl.BlockSpec((None, width, page), lambda b, j, pt: (b, 0, 0))
    seq_v = pl.BlockSpec((None, page * n_heads, V_DIM), lambda b, j, pt: (b, 0, 0))
    rep = (jnp.arange(page * n_heads)[None, :] // n_heads == jnp.arange(page)[:, None]).astype(BF16)

    def page_spec(shape, i):
        return pl.BlockSpec((None,) + shape, lambda b, j, pt: (pt[b, j * pages + i], 0, 0))

    grid_spec = pltpu.PrefetchScalarGridSpec(
        num_scalar_prefetch=1,
        grid=(n_seq, n_pages // pages),
        in_specs=[seq_k]
        + [page_spec((width, page), i) for i in range(pages)]
        + [page_spec((page * n_heads, V_DIM), i) for i in range(pages)]
        + [seq_k, seq_v,
           pl.BlockSpec((page, page * n_heads), lambda b, j, pt: (0, 0)),
           pl.BlockSpec((4, QK_DIM), lambda b, j, pt: (0, 0)),
           pl.BlockSpec((1, V_DIM), lambda b, j, pt: (0, 0))],
        out_specs=pl.BlockSpec((None, n_heads, V_DIM), lambda b, j, pt: (b, 0, 0)),
        scratch_shapes=[
            pltpu.VMEM((2 * n_heads, page), F32),
            pltpu.VMEM((2 * n_heads, 1), F32), pltpu.VMEM((2 * n_heads, 1), F32),
            pltpu.VMEM((2 * n_heads, V_DIM), F32),
        ],
    )
    return pl.pallas_call(
        functools.partial(_decode_kernel, pages=pages, n_heads=n_heads, lam_init=lam_init),
        grid_spec=grid_spec,
        out_shape=jax.ShapeDtypeStruct((n_seq, n_heads, V_DIM), BF16),
        compiler_params=_cparams(("parallel", "arbitrary")),
        name="diff_decode",
    )(page_table, qcol, *([cache_kt] * pages), *([cache_v2] * pages), kn, vn, rep, lp, sg.reshape(1, V_DIM))


def _prep_weights(ffn_w_in, ffn_w_out, s5, s5_w_glu, w_kv, attn_w_q, attn_w_o, n_pow):
    d_ff = ffn_w_in.shape[3] // 2
    main = d_ff // FF_TILE * FF_TILE
    assert main > 0
    wa = ffn_w_in[..., :main].astype(BF16)
    wb = ffn_w_in[..., d_ff:d_ff + main].astype(BF16)
    wo = ffn_w_out[:, :, :main, :].astype(BF16)
    tails = ()
    if main < d_ff:
        tails = (ffn_w_in[..., main:d_ff].astype(BF16), ffn_w_in[..., d_ff + main:].astype(BF16),
                 ffn_w_out[:, :, main:, :].astype(BF16))
    s5_prm = [_s5_params(*(t[l] for t in s5), n_pow) for l in range(s5_w_glu.shape[0])]
    return ((wa, wb, wo, tails), s5_prm, s5_w_glu.astype(BF16), w_kv.astype(BF16),
            attn_w_q.astype(BF16), attn_w_o.astype(BF16))


def _trunk(x, prompt, weights, gains, h0, kv_cache, page_table, diff_lambda, subln_g):
    ffn_w, s5_prm, glu_w, wkv, wq, wo = weights
    norm_g, kv_norm_g, final_norm_g = gains
    depth = norm_g.shape[0]
    n_a = len(s5_prm)
    if prompt:
        b, seq, d = x.shape
        m = b * seq
        n_t = seq // SUBLANES
        tm = min(ROW_TILE, n_t)
        tq = min(ROW_TILE, seq)
        blocks_per_sub = n_t // tm
        pos = jnp.arange(seq)

        def perm_index(i, j):
            return (i // (SUBLANES * blocks_per_sub), i % blocks_per_sub,
                    (i // blocks_per_sub) % SUBLANES)

        perm_spec = pl.BlockSpec((None, tm, d), perm_index)
    else:
        m, d = x.shape
        b, tm = 1, m
        pos = jnp.full((m,), page_table.shape[1] * kv_cache[0].shape[2], jnp.int32)
    tn = min(COL_TILE, d)
    qk_blocks = wq.shape[2] // tn
    tables = _rope_tables(pos)
    xf = x.reshape(m, d)
    plain = lambda dt: (jax.ShapeDtypeStruct((m, d), dt), pl.BlockSpec((tm, d), lambda i, j: (i, 0)))
    proj = functools.partial(_rope_proj, tables=tables, tm=tm, tn=tn, batch=b)
    states, k_out, v_out, k_att, v_att, hn = [], None, None, None, None, None
    for l in range(depth):
        if l == n_a:
            k_outs = ((BF16, False), (F32, True)) if prompt else ((F32, False),)
            v_outs = ((F32, False), (BF16, True)) if prompt else ((F32, False),)
            k_res = proj(hn, wkv, lambda j: (0, j), wq.shape[2], rope=True, outs=k_outs)
            v_res = proj(hn, wkv, lambda j: (0, qk_blocks + j), wkv.shape[1] - wq.shape[2], rope=False, outs=v_outs)
            k_att, k_out = k_res if prompt else (None, k_res[0])
            v_out, v_att = v_res if prompt else (v_res[0], None)
        if l < n_a and prompt:
            norm = (jax.ShapeDtypeStruct((b, n_t, SUBLANES * d), F32), perm_spec)
        else:
            norm = plain(F32 if l < n_a else BF16)
        xf, hn = _ffn(xf, norm_g[l, 0], ffn_w, (l, 0), norm_g[l, 1], tm=tm, norm=norm)
        if l < n_a:
            if prompt:
                g, st = _s5_scan(hn.reshape(b, seq, d), h0[l], s5_prm[l], rc=min(ROW_TILE, seq))
                a, a_spec = g.reshape(b, n_t, SUBLANES * d), pl.BlockSpec((None, tm, d), perm_index)
            else:
                a, st = _s5_single(hn, h0[l], s5_prm[l])
                a_spec = pl.BlockSpec((tm, d), lambda i, j: (i, 0))
            states.append(st)
            xf = _glu_proj(a, a_spec, glu_w, l, xf, tm=tm, tn=tn)
        else:
            jl = l - n_a
            lam_init = 0.8 - 0.6 * math.exp(-0.3 * l)
            q_outs = ((BF16, True),) if prompt else ((BF16, False),)
            q = proj(hn, wq, lambda j, jl=jl: (jl, 0, j), wq.shape[2], rope=True,
                     scale=Q_SCALE * LOG2_E, outs=q_outs)[0]
            if prompt:
                o = _flash(q, k_att.reshape(b, seq, -1), v_att, diff_lambda[jl], subln_g[jl],
                           tq=tq, lam_init=lam_init).reshape(m, -1)
            else:
                n_seq = page_table.shape[0]
                cache_kt, cache_v2 = kv_cache
                width, page = cache_kt.shape[1:]
                n_heads = width // V_DIM
                qcol = jnp.broadcast_to(q[:n_seq].astype(F32)[:, :, None], (n_seq, width, page))
                kn = jnp.zeros((n_seq, width, page), F32).at[:, :, 0].set(k_out[:n_seq])
                vn = jnp.zeros((n_seq, page * n_heads, V_DIM), F32).at[:, :n_heads, :].set(
                    v_out[:n_seq].reshape(n_seq, n_heads, V_DIM))
                o = _decode_attn(qcol, kn, vn, cache_kt, cache_v2, page_table,
                                 diff_lambda[jl], subln_g[jl], lam_init=lam_init)
                o = jnp.pad(o.reshape(n_seq, width), ((0, m - n_seq), (0, 0)))
            xf = _res_proj(o, wo, jl, xf, tm=tm, tn=tn)
        if l == n_a - 1:
            xf, hn = _ffn(xf, norm_g[l, 2], ffn_w, (l, 1), kv_norm_g, tm=tm, norm=plain(BF16))
        elif l == depth - 1:
            (y,) = _ffn(xf, norm_g[l, 2], ffn_w, (l, 1), final_norm_g, tm=tm, emit_x=False, norm=plain(F32))
        else:
            (xf,) = _ffn(xf, norm_g[l, 2], ffn_w, (l, 1), final_norm_g, tm=tm)
    return y, states, k_out, v_out


def kernel(x_prompt, x_sample, state_ssm_re, state_ssm_im, cache_k, cache_v, page_table, norm_g, ffn_w_in, ffn_w_out, s5_a_re, s5_a_im, s5_log_dt, s5_b_re, s5_b_im, s5_c_re, s5_c_im, s5_d, s5_w_glu, kv_norm_g, w_kv, attn_w_q, attn_w_o, diff_lambda, subln_g, final_norm_g):
    b, seq, d = x_prompt.shape
    n_seq = x_sample.shape[0]
    n_a, _, n_groups, p = state_ssm_re.shape
    n_oct = n_groups // OCT_GROUPS
    n_phys, page, n_heads = cache_v.shape[:3]
    n_t = seq // SUBLANES
    n_pow = n_t.bit_length() - 1
    assert seq == SUBLANES * (1 << n_pow) and x_sample.shape[1] == 1 and n_seq <= SAMPLE_ROWS

    s5 = (s5_a_re, s5_a_im, s5_log_dt, s5_b_re, s5_b_im, s5_c_re, s5_c_im, s5_d)
    weights = _prep_weights(ffn_w_in, ffn_w_out, s5, s5_w_glu, w_kv, attn_w_q, attn_w_o, n_pow)
    gains = (norm_g, kv_norm_g, final_norm_g)

    h0_p = jnp.zeros((n_a, b, n_oct, 1, 2 * OCT_STATE), F32)
    y_p, st_p, kt_p, v_p = _trunk(x_prompt, True, weights, gains, h0_p, None, None, diff_lambda, subln_g)
    k_p = jnp.transpose(kt_p.reshape(b, n_heads, 2, QK_DIM, seq), (0, 4, 1, 2, 3))

    def unblock_prompt(st, half):
        t = st[:, :, 0, half * OCT_STATE:(half + 1) * OCT_STATE]
        return t.reshape(b, n_groups, p)

    re_p = jnp.stack([unblock_prompt(s, 0) for s in st_p])
    im_p = jnp.stack([unblock_prompt(s, 1) for s in st_p])

    pad = SAMPLE_ROWS - n_seq
    xs = jnp.pad(x_sample.reshape(n_seq, d), ((0, pad), (0, 0)))

    def block_sample(t):
        t = t.reshape(n_a, n_seq, n_oct, OCT_STATE).transpose(0, 2, 1, 3)
        return jnp.pad(t, ((0, 0), (0, 0), (0, pad), (0, 0)))

    h0_s = jnp.concatenate([block_sample(state_ssm_re), block_sample(state_ssm_im)], axis=-1)
    kv_cache = (jnp.transpose(cache_k, (0, 2, 3, 4, 1)).reshape(n_phys, -1, page),
                cache_v.reshape(n_phys, page * n_heads, V_DIM))
    y_s, st_s, k_s, v_s = _trunk(xs, False, weights, gains, h0_s, kv_cache, page_table, diff_lambda, subln_g)

    def unblock_sample(st, half):
        t = st[:, :n_seq, half * OCT_STATE:(half + 1) * OCT_STATE]
        return t.transpose(1, 0, 2).reshape(n_seq, n_groups, p)

    re_s = jnp.stack([unblock_sample(s, 0) for s in st_s])
    im_s = jnp.stack([unblock_sample(s, 1) for s in st_s])

    return (y_p.reshape(b, seq, d), y_s[:n_seq].reshape(n_seq, 1, d), re_p, im_p,
            k_p, v_p.reshape(b, seq, n_heads, V_DIM),
            re_s, im_s,
            k_s[:n_seq].reshape(n_seq, 1, n_heads, 2, QK_DIM), v_s[:n_seq].reshape(n_seq, 1, n_heads, V_DIM))
```

```python
import functools
import math

import jax
import jax.numpy as jnp
from jax import lax
from jax.experimental import pallas as pl
from jax.experimental.pallas import tpu as pltpu

F32 = jnp.float32
BF16 = jnp.bfloat16

GROUP_CH = 16
STATE_DIM = 64
QK_DIM = 64
V_DIM = 2 * QK_DIM
ROT_DIM = QK_DIM // 4
ROPE_THETA = 500000.0
NORM_EPS = 1e-6
NEG_INF = -1e30
Q_SCALE = QK_DIM ** -0.5
LOG2_E = 1.4426950408889634

LANES = 128
SUBLANES = 8
OCT_GROUPS = LANES // GROUP_CH
OCT_STATE = OCT_GROUPS * STATE_DIM
VMEM_LIMIT = 56 * 1024 * 1024

ROW_TILE = 512
COL_TILE = 2048
GLU_COL_TILE = 1024
FF_TILE = 512
FLASH_HEADS = 2
SAMPLE_ROWS = 16
PAGES_PER_STEP = 4


def _cparams(sem):
    return pltpu.CompilerParams(dimension_semantics=sem, vmem_limit_bytes=VMEM_LIMIT)


def _rms(x, g):
    return x * lax.rsqrt(jnp.mean(x * x, axis=-1, keepdims=True) + NORM_EPS) * g


def _sigmoid(x):
    return 1.0 / (1.0 + jnp.exp(-x))


def _ffn_kernel(x_ref, g_ref, gn_ref, wa_ref, wb_ref, wo_ref, *rest, n_main, has_tail, emit_x, emit_norm):
    tails, rest = (rest[:3], rest[3:]) if has_tail else ((), rest)
    outs, (h_scr, acc_scr) = rest[:-2], rest[-2:]
    j = pl.program_id(1)

    @pl.when(j == 0)
    def _():
        h_scr[...] = _rms(x_ref[...], g_ref[...]).astype(BF16)
        acc_scr[...] = jnp.zeros(acc_scr.shape, F32)

    def hidden(wa, wb, wo):
        h = h_scr[...]
        za = jnp.dot(h, wa[0, 0], preferred_element_type=F32)
        zb = jnp.dot(h, wb[0, 0], preferred_element_type=F32)
        act = (za * _sigmoid(za)) * zb
        return jnp.dot(act.astype(BF16), wo[0, 0], preferred_element_type=F32)

    acc_scr[...] += hidden(wa_ref, wb_ref, wo_ref)

    @pl.when(j == n_main - 1)
    def _():
        acc = acc_scr[...]
        if has_tail:
            acc = acc + hidden(*tails)
        xo = x_ref[...] + 0.5 * acc
        k = 0
        if emit_x:
            outs[k][...] = xo
            k += 1
        if emit_norm:
            outs[k][...] = _rms(xo, gn_ref[...]).astype(outs[k].dtype)


def _ffn(x, g, w, lk, gn, *, tm, emit_x=True, norm=None):
    m, d = x.shape
    l, k = lk
    w_in, w_out = w
    d_ff = w_out.shape[2]
    n_main = d_ff // FF_TILE
    main = n_main * FF_TILE
    tail = d_ff - main
    out_shape, out_specs = [], []
    if emit_x:
        out_shape.append(jax.ShapeDtypeStruct((m, d), F32))
        out_specs.append(pl.BlockSpec((tm, d), lambda i, j: (i, 0)))
    if norm is not None:
        out_shape.append(norm[0])
        out_specs.append(norm[1])
    row = pl.BlockSpec((1, d), lambda i, j: (0, 0))
    one, full = pl.Element(1), pl.Element(d)

    assert d_ff % LANES == 0 and FF_TILE % LANES == 0

    def offset(j, start, step):
        return (start // LANES + j * (step // LANES)) * LANES

    def cols(size, start, step):
        return pl.BlockSpec((one, one, full, pl.Element(size)),
                            lambda i, j: (l, k, 0, offset(j, start, step)))

    def rows(size, start, step):
        return pl.BlockSpec((one, one, pl.Element(size), full),
                            lambda i, j: (l, k, offset(j, start, step), 0))

    in_specs = [pl.BlockSpec((tm, d), lambda i, j: (i, 0)), row, row,
                cols(FF_TILE, 0, FF_TILE), cols(FF_TILE, d_ff, FF_TILE), rows(FF_TILE, 0, FF_TILE)]
    operands = [w_in, w_in, w_out]
    if tail:
        in_specs += [cols(tail, main, 0), cols(tail, d_ff + main, 0), rows(tail, main, 0)]
        operands += [w_in, w_in, w_out]
    return pl.pallas_call(
        functools.partial(_ffn_kernel, n_main=n_main, has_tail=bool(tail), emit_x=emit_x,
                          emit_norm=norm is not None),
        grid=(m // tm, n_main),
        in_specs=in_specs,
        out_specs=out_specs,
        out_shape=out_shape,
        scratch_shapes=[pltpu.VMEM((tm, d), BF16), pltpu.VMEM((tm, d), F32)],
        compiler_params=_cparams(("parallel", "arbitrary")),
        name="ffn",
    )(x, g.reshape(1, d), gn.reshape(1, d), *operands)


def _glu_kernel(a_ref, w1_ref, w2_ref, res_ref, o_ref):
    a = a_ref[...]
    z1 = jnp.dot(a, w1_ref[...], preferred_element_type=F32)
    z2 = jnp.dot(a, w2_ref[...], preferred_element_type=F32)
    o_ref[...] = res_ref[...] + z1 * _sigmoid(z2)


def _glu_proj(a, w, l, res, *, tm, tn):
    m, n = res.shape
    kdim = w.shape[1]
    nb = n // tn
    return pl.pallas_call(
        _glu_kernel,
        grid=(nb, m // tm),
        in_specs=[
            pl.BlockSpec((tm, kdim), lambda j, i: (i, 0)),
            pl.BlockSpec((None, kdim, tn), lambda j, i: (l, 0, j)),
            pl.BlockSpec((None, kdim, tn), lambda j, i: (l, 0, nb + j)),
            pl.BlockSpec((tm, tn), lambda j, i: (i, j)),
        ],
        out_specs=pl.BlockSpec((tm, tn), lambda j, i: (i, j)),
        out_shape=jax.ShapeDtypeStruct((m, n), F32),
        compiler_params=_cparams(("parallel", "arbitrary")),
        name="s5_glu",
    )(a, w, w, res)


def _res_proj_kernel(a_ref, w_ref, res_ref, o_ref):
    o_ref[...] = res_ref[...] + jnp.dot(a_ref[...], w_ref[...], preferred_element_type=F32)


def _res_proj(a, w, l, res, *, tm, tn):
    m, n = res.shape
    kdim = w.shape[1]
    return pl.pallas_call(
        _res_proj_kernel,
        grid=(n // tn, m // tm),
        in_specs=[
            pl.BlockSpec((tm, kdim), lambda j, i: (i, 0)),
            pl.BlockSpec((None, kdim, tn), lambda j, i: (l, 0, j)),
            pl.BlockSpec((tm, tn), lambda j, i: (i, j)),
        ],
        out_specs=pl.BlockSpec((tm, tn), lambda j, i: (i, j)),
        out_shape=jax.ShapeDtypeStruct((m, n), F32),
        compiler_params=_cparams(("parallel", "arbitrary")),
        name="attn_out_proj",
    )(a, w, res)


def _rope_proj_kernel(a_ref, w_ref, cos_ref, sa_ref, sb_ref, *outs, rope, scale, transposed):
    z = jnp.dot(a_ref[...], w_ref[...], preferred_element_type=F32)
    if rope:
        cos, sa, sb = cos_ref[...], sa_ref[...], sb_ref[...]
        cols = []
        for c in range(z.shape[1] // LANES):
            zc = z[:, c * LANES:(c + 1) * LANES]
            up = pltpu.roll(zc, LANES - ROT_DIM // 2, 1)
            dn = pltpu.roll(zc, ROT_DIM // 2, 1)
            cols.append(zc * cos + up * sa + dn * sb)
        z = jnp.concatenate(cols, axis=1)
    if scale != 1.0:
        z = z * scale
    zt = z.T if any(transposed) else None
    for o, t in zip(outs, transposed):
        o[...] = (zt if t else z).astype(o.dtype)


def _rope_proj(a, w, w_index, n, tables, *, tm, tn, rope, scale=1.0, outs=((BF16, False),), batch=1):
    m, kdim = a.shape
    n_tab = tables[0].shape[0] // tm
    tab_spec = pl.BlockSpec((tm, LANES), lambda j, i: (i % n_tab, 0))
    bpb = m // batch // tm
    out_specs, out_shape = [], []
    for dt, t in outs:
        if t:
            out_shape.append(jax.ShapeDtypeStruct((batch, n, m // batch), dt))
            out_specs.append(pl.BlockSpec((None, tn, tm), lambda j, i: (i // bpb, j, i % bpb)))
        else:
            out_shape.append(jax.ShapeDtypeStruct((m, n), dt))
            out_specs.append(pl.BlockSpec((tm, tn), lambda j, i: (i, j)))
    w_block = (None,) * (w.ndim - 2) + (kdim, tn)
    return pl.pallas_call(
        functools.partial(_rope_proj_kernel, rope=rope, scale=scale, transposed=tuple(t for _, t in outs)),
        grid=(n // tn, m // tm),
        in_specs=[
            pl.BlockSpec((tm, kdim), lambda j, i: (i, 0)),
            pl.BlockSpec(w_block, lambda j, i: w_index(j)),
            tab_spec, tab_spec, tab_spec,
        ],
        out_specs=out_specs,
        out_shape=out_shape,
        compiler_params=_cparams(("parallel", "arbitrary")),
        name="rope_proj" if rope else "plain_proj",
    )(a, w, *tables)


def _rope_tables(pos):
    inv = ROPE_THETA ** (-jnp.arange(0, ROT_DIM, 2, dtype=F32) / ROT_DIM)
    ang = pos.astype(F32)[:, None] * inv[None, :]
    cos, sin = jnp.cos(ang), jnp.sin(ang)
    half = ROT_DIM // 2
    n = pos.shape[0]
    ones = jnp.ones((n, QK_DIM - ROT_DIM), F32)
    zeros = jnp.zeros((n, QK_DIM - half), F32)
    cos64 = jnp.concatenate([cos, cos, ones], axis=1)
    sa64 = jnp.concatenate([-sin, zeros], axis=1)
    sb64 = jnp.concatenate([jnp.zeros((n, half), F32), sin, jnp.zeros((n, QK_DIM - ROT_DIM), F32)], axis=1)
    rep = LANES // QK_DIM
    return tuple(jnp.tile(t, (1, rep)) for t in (cos64, sa64, sb64))


def _s5_step(ar, ai, hr, hi, xr, xi):
    return (ar * hr + xr) - ai * hi, (ar * hi + xi) + ai * hr


def _s5_scan_kernel(u_ref, h0_ref, bw_ref, cw_ref, a_ref, d_ref, g_ref, ht_ref,
                    x_scr, f_scr, hs_scr, *, seq, rc):
    n_t = seq // SUBLANES
    chunks = [(s, t0) for s in range(SUBLANES) for t0 in range(0, n_t, rc)]
    p = OCT_STATE
    nc = p // LANES
    bw = bw_ref[...]

    def lanes(c):
        return slice(c * LANES, (c + 1) * LANES)

    for s, t0 in chunks:
        x = jnp.dot(u_ref[pl.ds(s * n_t + t0, rc), :].astype(BF16), bw, preferred_element_type=F32)
        for c in range(2 * nc):
            x_scr[c, pl.ds(t0 * SUBLANES + s, rc, stride=SUBLANES), :] = x[:, lanes(c)]

    a = a_ref[...]
    ar = [jnp.broadcast_to(a[0:1, lanes(c)], (SUBLANES, LANES)) for c in range(nc)]
    ai = [jnp.broadcast_to(a[1:2, lanes(c)], (SUBLANES, LANES)) for c in range(nc)]

    def make_step(store):
        def step(t, carry):
            r0 = pl.multiple_of(t * SUBLANES, SUBLANES)
            new = []
            for c in range(nc):
                nr, ni = _s5_step(ar[c], ai[c], carry[c][0], carry[c][1],
                                  x_scr[c, pl.ds(r0, SUBLANES), :], x_scr[nc + c, pl.ds(r0, SUBLANES), :])
                if store:
                    x_scr[c, pl.ds(r0, SUBLANES), :] = nr
                    x_scr[nc + c, pl.ds(r0, SUBLANES), :] = ni
                new.append((nr, ni))
            return tuple(new)
        return step

    zero = jnp.zeros((SUBLANES, LANES), F32)
    fin = lax.fori_loop(0, n_t, make_step(False), tuple((zero, zero) for _ in range(nc)), unroll=8)
    for c in range(nc):
        f_scr[:, lanes(c)] = fin[c][0]
        f_scr[:, lanes(nc + c)] = fin[c][1]

    atr, ati = a[2:3], a[3:4]
    pr, pi = h0_ref[:, 0:p], h0_ref[:, p:2 * p]
    hs_scr[0:1, 0:p] = pr
    hs_scr[0:1, p:2 * p] = pi
    for s in range(1, SUBLANES):
        pr, pi = _s5_step(atr, ati, pr, pi, f_scr[s - 1:s, 0:p], f_scr[s - 1:s, p:2 * p])
        hs_scr[s:s + 1, 0:p] = pr
        hs_scr[s:s + 1, p:2 * p] = pi

    start = tuple((hs_scr[:, lanes(c)], hs_scr[:, lanes(nc + c)]) for c in range(nc))
    fin = lax.fori_loop(0, n_t, make_step(True), start, unroll=8)
    for c in range(nc):
        ht_ref[:, lanes(c)] = fin[c][0][SUBLANES - 1:SUBLANES]
        ht_ref[:, lanes(nc + c)] = fin[c][1][SUBLANES - 1:SUBLANES]

    cw = cw_ref[...]
    d = d_ref[...]

    for s, t0 in chunks:
        rows = pl.ds(s * n_t + t0, rc)
        h = jnp.concatenate([x_scr[c, pl.ds(t0 * SUBLANES + s, rc, stride=SUBLANES), :].astype(BF16)
                             for c in range(2 * nc)], axis=1)
        y = jnp.dot(h, cw, preferred_element_type=F32) + d * u_ref[rows, :]
        g_ref[rows, :] = jax.nn.gelu(y).astype(BF16)


def _s5_scan(u, h0, prm, *, rc):
    b, seq, d = u.shape
    n_oct = d // LANES
    p2 = 2 * OCT_STATE
    bw, cw, arows, drow = prm
    return pl.pallas_call(
        functools.partial(_s5_scan_kernel, seq=seq, rc=rc),
        grid=(b, n_oct),
        in_specs=[
            pl.BlockSpec((None, seq, LANES), lambda i, o: (i, 0, o)),
            pl.BlockSpec((None, None, 1, p2), lambda i, o: (i, o, 0, 0)),
            pl.BlockSpec((None, LANES, p2), lambda i, o: (o, 0, 0)),
            pl.BlockSpec((None, p2, LANES), lambda i, o: (o, 0, 0)),
            pl.BlockSpec((None, 4, OCT_STATE), lambda i, o: (o, 0, 0)),
            pl.BlockSpec((None, 1, LANES), lambda i, o: (o, 0, 0)),
        ],
        out_specs=[
            pl.BlockSpec((None, seq, LANES), lambda i, o: (i, 0, o)),
            pl.BlockSpec((None, None, 1, p2), lambda i, o: (i, o, 0, 0)),
        ],
        out_shape=[
            jax.ShapeDtypeStruct((b, seq, d), BF16),
            jax.ShapeDtypeStruct((b, n_oct, 1, p2), F32),
        ],
        scratch_shapes=[
            pltpu.VMEM((p2 // LANES, seq, LANES), F32),
            pltpu.VMEM((SUBLANES, p2), F32),
            pltpu.VMEM((SUBLANES, p2), F32),
        ],
        compiler_params=_cparams(("parallel", "arbitrary")),
        name="s5_scan",
    )(u, h0, bw, cw, arows, drow)


def _s5_single_kernel(u_ref, h0_ref, bw_ref, cw_ref, a_ref, d_ref, g_ref, hn_ref):
    p = OCT_STATE
    u = u_ref[...]
    x = jnp.dot(u.astype(BF16), bw_ref[...], preferred_element_type=F32)
    a = a_ref[...]
    hr, hi = _s5_step(a[0:1], a[1:2], h0_ref[:, 0:p], h0_ref[:, p:2 * p], x[:, 0:p], x[:, p:2 * p])
    hn_ref[:, 0:p] = hr
    hn_ref[:, p:2 * p] = hi
    y = jnp.dot(hn_ref[...].astype(BF16), cw_ref[...], preferred_element_type=F32) + d_ref[...] * u
    g_ref[...] = jax.nn.gelu(y).astype(BF16)


def _s5_single(u, h0, prm):
    m, d = u.shape
    n_oct = d // LANES
    p2 = 2 * OCT_STATE
    bw, cw, arows, drow = prm
    return pl.pallas_call(
        _s5_single_kernel,
        grid=(n_oct,),
        in_specs=[
            pl.BlockSpec((m, LANES), lambda o: (0, o)),
            pl.BlockSpec((None, m, p2), lambda o: (o, 0, 0)),
            pl.BlockSpec((None, LANES, p2), lambda o: (o, 0, 0)),
            pl.BlockSpec((None, p2, LANES), lambda o: (o, 0, 0)),
            pl.BlockSpec((None, 4, OCT_STATE), lambda o: (o, 0, 0)),
            pl.BlockSpec((None, 1, LANES), lambda o: (o, 0, 0)),
        ],
        out_specs=[
            pl.BlockSpec((m, LANES), lambda o: (0, o)),
            pl.BlockSpec((None, m, p2), lambda o: (o, 0, 0)),
        ],
        out_shape=[
            jax.ShapeDtypeStruct((m, d), BF16),
            jax.ShapeDtypeStruct((n_oct, m, p2), F32),
        ],
        compiler_params=_cparams(("parallel",)),
        name="s5_single",
    )(u, h0, bw, cw, arows, drow)


def _s5_params(a_re, a_im, log_dt, b_re, b_im, c_re, c_im, d, n_pow):
    g = a_re.shape[0]
    n_oct = g // OCT_GROUPS
    ar, ai = a_re.astype(F32), a_im.astype(F32)
    dt = jnp.exp(log_dt.astype(F32))[:, None]
    mag = jnp.exp(dt * ar)
    ab_re = mag * jnp.cos(dt * ai)
    ab_im = mag * jnp.sin(dt * ai)
    den = ar * ar + ai * ai
    n_re = ab_re - 1.0
    n_im = ab_im
    f_re = (n_re * ar + n_im * ai) / den
    f_im = (n_im * ar - n_re * ai) / den
    br, bi = b_re.astype(F32), b_im.astype(F32)
    bb_re = f_re[..., None] * br - f_im[..., None] * bi
    bb_im = f_re[..., None] * bi + f_im[..., None] * br
    eye = jnp.eye(OCT_GROUPS, dtype=F32)

    def in_blocks(bb):
        t = bb.reshape(n_oct, OCT_GROUPS, STATE_DIM, GROUP_CH)
        t = jnp.einsum('ogpc,gh->ogchp', t, eye)
        return t.reshape(n_oct, LANES, OCT_STATE)

    def out_blocks(cc):
        t = cc.astype(F32).reshape(n_oct, OCT_GROUPS, GROUP_CH, STATE_DIM)
        t = jnp.einsum('ogcp,gh->ogphc', t, eye)
        return t.reshape(n_oct, OCT_STATE, LANES)

    bw = jnp.concatenate([in_blocks(bb_re), in_blocks(bb_im)], axis=2).astype(BF16)
    cw = jnp.concatenate([out_blocks(c_re), -out_blocks(c_im)], axis=1).astype(BF16)
    pw_re, pw_im = ab_re, ab_im
    for _ in range(n_pow):
        pw_re, pw_im = pw_re * pw_re - pw_im * pw_im, 2.0 * pw_re * pw_im
    arows = jnp.stack([t.reshape(n_oct, OCT_STATE) for t in (ab_re, ab_im, pw_re, pw_im)], axis=1)
    drow = d.astype(F32).reshape(n_oct, 1, LANES)
    return bw, cw, arows, drow


def _lambda(lp, lam_init):
    a = jnp.sum(lp[0:1] * lp[1:2], axis=1, keepdims=True)
    b = jnp.sum(lp[2:3] * lp[3:4], axis=1, keepdims=True)
    return jnp.exp(a) - jnp.exp(b) + lam_init


def _flash_kernel(qtab, ktab, qt_ref, k_ref, vt_ref, lp_ref, sg_ref, o_ref, m_scr, l_scr, a_scr, *, lam_init):
    step = pl.program_id(2)
    qi, ki = qtab[step], ktab[step]
    heads = k_ref.shape[1] // LANES

    @pl.when(ki == 0)
    def _():
        m_scr[...] = jnp.full(m_scr.shape, NEG_INF, F32)
        l_scr[...] = jnp.zeros(l_scr.shape, F32)
        a_scr[...] = jnp.zeros(a_scr.shape, F32)

    def block(diagonal):
        tk, tq = k_ref.shape[0], qt_ref.shape[1]
        feat = lax.broadcasted_iota(jnp.int32, (LANES, tq), 0)
        if diagonal:
            keep = (lax.broadcasted_iota(jnp.int32, (tk, tq), 0)
                    <= lax.broadcasted_iota(jnp.int32, (tk, tq), 1))
        scores = []
        for h in range(heads):
            qt = qt_ref[h * LANES:(h + 1) * LANES, :]
            k = k_ref[:, h * LANES:(h + 1) * LANES]
            zero = jnp.zeros_like(qt)
            for qm in (jnp.where(feat < QK_DIM, qt, zero), jnp.where(feat >= QK_DIM, qt, zero)):
                scores.append(jnp.dot(k, qm, preferred_element_type=F32))
        probs, alphas = [], []
        for c, st in enumerate(scores):
            if diagonal:
                st = jnp.where(keep, st, NEG_INF)
            m_old = m_scr[c:c + 1, :]
            m_new = jnp.maximum(m_old, jnp.max(st, axis=0, keepdims=True))
            alpha = jnp.exp2(m_old - m_new)
            pt = jnp.exp2(st - m_new)
            l_scr[c:c + 1, :] = alpha * l_scr[c:c + 1, :] + jnp.sum(pt, axis=0, keepdims=True)
            m_scr[c:c + 1, :] = m_new
            probs.append(pt.astype(BF16))
            alphas.append(alpha)
        for c, (pt, alpha) in enumerate(zip(probs, alphas)):
            h = c // 2
            rows = slice(c * V_DIM, (c + 1) * V_DIM)
            vt = vt_ref[h * LANES:(h + 1) * LANES, :]
            a_scr[rows, :] = alpha * a_scr[rows, :] + jnp.dot(vt, pt, preferred_element_type=F32)

    @pl.when(ki < qi)
    def _():
        block(False)

    @pl.when(ki == qi)
    def _():
        block(True)
        lam = _lambda(lp_ref[...], lam_init)
        for h in range(heads):
            c = 2 * h
            o1 = a_scr[c * V_DIM:(c + 1) * V_DIM, :] / l_scr[c:c + 1, :]
            o2 = a_scr[(c + 1) * V_DIM:(c + 2) * V_DIM, :] / l_scr[c + 1:c + 2, :]
            ot = o1 - lam * o2
            ms = jnp.mean(ot * ot, axis=0, keepdims=True)
            ot = ot * lax.rsqrt(ms + NORM_EPS) * sg_ref[...] * (1.0 - lam_init)
            o_ref[:, h * LANES:(h + 1) * LANES] = ot.T.astype(o_ref.dtype)


def _flash(qt, k, vt, lp, sg, *, tq, lam_init):
    b, seq, width = k.shape
    n_heads = width // LANES
    hp = FLASH_HEADS if n_heads % FLASH_HEADS == 0 else 1
    hw = hp * LANES
    nq = seq // tq
    pairs = [(i, j) for i in range(nq) for j in range(i + 1)]
    qtab = jnp.asarray([p[0] for p in pairs], jnp.int32)
    ktab = jnp.asarray([p[1] for p in pairs], jnp.int32)
    grid_spec = pltpu.PrefetchScalarGridSpec(
        num_scalar_prefetch=2,
        grid=(b, n_heads // hp, len(pairs)),
        in_specs=[
            pl.BlockSpec((None, hw, tq), lambda i, h, s, qt_, kt_: (i, h, qt_[s])),
            pl.BlockSpec((None, tq, hw), lambda i, h, s, qt_, kt_: (i, kt_[s], h)),
            pl.BlockSpec((None, hw, tq), lambda i, h, s, qt_, kt_: (i, h, kt_[s])),
            pl.BlockSpec((4, QK_DIM), lambda i, h, s, qt_, kt_: (0, 0)),
            pl.BlockSpec((V_DIM, 1), lambda i, h, s, qt_, kt_: (0, 0)),
        ],
        out_specs=pl.BlockSpec((None, tq, hw), lambda i, h, s, qt_, kt_: (i, qt_[s], h)),
        scratch_shapes=[
            pltpu.VMEM((2 * hp, tq), F32), pltpu.VMEM((2 * hp, tq), F32),
            pltpu.VMEM((2 * hp * V_DIM, tq), F32),
        ],
    )
    return pl.pallas_call(
        functools.partial(_flash_kernel, lam_init=lam_init),
        grid_spec=grid_spec,
        out_shape=jax.ShapeDtypeStruct((b, seq, width), BF16),
        compiler_params=_cparams(("parallel", "parallel", "arbitrary")),
        name="diff_flash",
    )(qtab, ktab, qt, k, vt, lp, sg.reshape(V_DIM, 1))


def _decode_kernel(pt_ref, qcol_ref, *refs, pages, n_heads, lam_init):
    k_refs, v_refs = refs[:pages], refs[pages:2 * pages]
    kn_ref, vn_ref, rep_ref, lp_ref, sg_ref, o_ref, s_scr, m_scr, l_scr, acc_scr = refs[2 * pages:]
    j = pl.program_id(1)
    rows = 2 * n_heads
    n_tok = kn_ref.shape[1]

    @pl.when(j == 0)
    def _():
        m_scr[...] = jnp.full(m_scr.shape, NEG_INF, F32)
        l_scr[...] = jnp.zeros(l_scr.shape, F32)
        acc_scr[...] = jnp.zeros(acc_scr.shape, F32)

    rowi = lax.broadcasted_iota(jnp.int32, (rows, n_tok * n_heads), 0)
    coli = lax.broadcasted_iota(jnp.int32, (rows, n_tok * n_heads), 1)
    head_sel = (coli & (n_heads - 1)) == (rowi >> 1)

    def page(kt_ref, v_ref, n_valid):
        prod = kt_ref[...] * qcol_ref[...]
        for c in range(rows):
            s_scr[c:c + 1, :] = jnp.sum(prod[c * QK_DIM:(c + 1) * QK_DIM, :], axis=0, keepdims=True)
        s = s_scr[...]
        if n_valid is not None:
            s = jnp.where(lax.broadcasted_iota(jnp.int32, s.shape, 1) < n_valid, s, NEG_INF)
        m_old = m_scr[...]
        m_new = jnp.maximum(m_old, jnp.max(s, axis=1, keepdims=True))
        alpha = jnp.exp2(m_old - m_new)
        p = jnp.exp2(s - m_new)
        l_scr[...] = alpha * l_scr[...] + jnp.sum(p, axis=1, keepdims=True)
        spread = jnp.dot(p.astype(BF16), rep_ref[...], preferred_element_type=F32)
        spread = jnp.where(head_sel, spread, 0.0).astype(BF16)
        pv = jnp.dot(spread, v_ref[...].astype(BF16), preferred_element_type=F32)
        acc_scr[...] = alpha * acc_scr[...] + pv
        m_scr[...] = m_new

    for i in range(pages):
        page(k_refs[i], v_refs[i], None)

    @pl.when(j == pl.num_programs(1) - 1)
    def _():
        page(kn_ref, vn_ref, 1)
        lam = _lambda(lp_ref[...], lam_init)
        s_scr[...] = acc_scr[...] / l_scr[...]
        o = s_scr[pl.ds(0, n_heads, stride=2), :] - lam * s_scr[pl.ds(1, n_heads, stride=2), :]
        o_ref[...] = (_rms(o, sg_ref[...]) * (1.0 - lam_init)).astype(o_ref.dtype)


def _decode_attn(qcol, kn, vn, cache_kt, cache_v2, page_table, lp, sg, *, lam_init):
    n_seq, n_pages = page_table.shape
    _, width, page = cache_kt.shape
    n_heads = width // V_DIM
    assert n_heads & (n_heads - 1) == 0 and page == LANES
    pages = PAGES_PER_STEP if n_pages % PAGES_PER_STEP == 0 else 1
    seq_k = pl.BlockSpec((None, width, page), lambda b, j, pt: (b, 0, 0))
    seq_v = pl.BlockSpec((None, page * n_heads, V_DIM), lambda b, j, pt: (b, 0, 0))
    rep = (jnp.arange(page * n_heads)[None, :] // n_heads == jnp.arange(page)[:, None]).astype(BF16)

    def page_spec(shape, i):
        return pl.BlockSpec((None,) + shape, lambda b, j, pt: (pt[b, j * pages + i], 0, 0))

    grid_spec = pltpu.PrefetchScalarGridSpec(
        num_scalar_prefetch=1,
        grid=(n_seq, n_pages // pages),
        in_specs=[seq_k]
        + [page_spec((width, page), i) for i in range(pages)]
        + [page_spec((page * n_heads, V_DIM), i) for i in range(pages)]
        + [seq_k, seq_v,
           pl.BlockSpec((page, page * n_heads), lambda b, j, pt: (0, 0)),
           pl.BlockSpec((4, QK_DIM), lambda b, j, pt: (0, 0)),
           pl.BlockSpec((1, V_DIM), lambda b, j, pt: (0, 0))],
        out_specs=pl.BlockSpec((None, n_heads, V_DIM), lambda b, j, pt: (b, 0, 0)),
        scratch_shapes=[
            pltpu.VMEM((2 * n_heads, page), F32),
            pltpu.VMEM((2 * n_heads, 1), F32), pltpu.VMEM((2 * n_heads, 1), F32),
            pltpu.VMEM((2 * n_heads, V_DIM), F32),
        ],
    )
    return pl.pallas_call(
        functools.partial(_decode_kernel, pages=pages, n_heads=n_heads, lam_init=lam_init),
        grid_spec=grid_spec,
        out_shape=jax.ShapeDtypeStruct((n_seq, n_heads, V_DIM), BF16),
        compiler_params=_cparams(("parallel", "arbitrary")),
        name="diff_decode",
    )(page_table, qcol, *([cache_kt] * pages), *([cache_v2] * pages), kn, vn, rep, lp, sg.reshape(1, V_DIM))


def _prep_weights(ffn_w_in, ffn_w_out, s5, s5_w_glu, w_kv, attn_w_q, attn_w_o, n_pow):
    assert ffn_w_out.shape[2] >= FF_TILE
    s5_prm = [_s5_params(*(t[l] for t in s5), n_pow) for l in range(s5_w_glu.shape[0])]
    return ((ffn_w_in.astype(BF16), ffn_w_out.astype(BF16)), s5_prm, s5_w_glu.astype(BF16),
            w_kv.astype(BF16), attn_w_q.astype(BF16), attn_w_o.astype(BF16))


def _trunk(x, prompt, weights, gains, h0, kv_cache, page_table, diff_lambda, subln_g):
    ffn_w, s5_prm, glu_w, wkv, wq, wo = weights
    norm_g, kv_norm_g, final_norm_g = gains
    depth = norm_g.shape[0]
    n_a = len(s5_prm)
    if prompt:
        b, seq, d = x.shape
        m = b * seq
        tm = tq = min(ROW_TILE, seq // SUBLANES)
        pos = jnp.arange(seq)
    else:
        m, d = x.shape
        b, tm = 1, m
        pos = jnp.full((m,), page_table.shape[1] * kv_cache[0].shape[2], jnp.int32)
    tn = min(COL_TILE, d)
    qk_blocks = wq.shape[2] // tn
    tables = _rope_tables(pos)
    xf = x.reshape(m, d)
    plain = lambda dt: (jax.ShapeDtypeStruct((m, d), dt), pl.BlockSpec((tm, d), lambda i, j: (i, 0)))
    proj = functools.partial(_rope_proj, tables=tables, tm=tm, tn=tn, batch=b)
    states, k_out, v_out, k_att, v_att, hn = [], None, None, None, None, None
    for l in range(depth):
        if l == n_a:
            k_outs = ((BF16, False), (F32, True)) if prompt else ((F32, False),)
            v_outs = ((F32, False), (BF16, True)) if prompt else ((F32, False),)
            k_res = proj(hn, wkv, lambda j: (0, j), wq.shape[2], rope=True, outs=k_outs)
            v_res = proj(hn, wkv, lambda j: (0, qk_blocks + j), wkv.shape[1] - wq.shape[2], rope=False, outs=v_outs)
            k_att, k_out = k_res if prompt else (None, k_res[0])
            v_out, v_att = v_res if prompt else (v_res[0], None)
        xf, hn = _ffn(xf, norm_g[l, 0], ffn_w, (l, 0), norm_g[l, 1], tm=tm, norm=plain(F32 if l < n_a else BF16))
        if l < n_a:
            if prompt:
                g, st = _s5_scan(hn.reshape(b, seq, d), h0[l], s5_prm[l], rc=tm)
                g = g.reshape(m, d)
            else:
                g, st = _s5_single(hn, h0[l], s5_prm[l])
            states.append(st)
            xf = _glu_proj(g, glu_w, l, xf, tm=tm, tn=min(GLU_COL_TILE, d))
        else:
            jl = l - n_a
            lam_init = 0.8 - 0.6 * math.exp(-0.3 * l)
            q_outs = ((BF16, True),) if prompt else ((BF16, False),)
            q = proj(hn, wq, lambda j, jl=jl: (jl, 0, j), wq.shape[2], rope=True,
                     scale=Q_SCALE * LOG2_E, outs=q_outs)[0]
            if prompt:
                o = _flash(q, k_att.reshape(b, seq, -1), v_att, diff_lambda[jl], subln_g[jl],
                           tq=tq, lam_init=lam_init).reshape(m, -1)
            else:
                n_seq = page_table.shape[0]
                cache_kt, cache_v2 = kv_cache
                width, page = cache_kt.shape[1:]
                n_heads = width // V_DIM
                qcol = jnp.broadcast_to(q[:n_seq].astype(F32)[:, :, None], (n_seq, width, page))
                kn = jnp.zeros((n_seq, width, page), F32).at[:, :, 0].set(k_out[:n_seq])
                vn = jnp.zeros((n_seq, page * n_heads, V_DIM), F32).at[:, :n_heads, :].set(
                    v_out[:n_seq].reshape(n_seq, n_heads, V_DIM))
                o = _decode_attn(qcol, kn, vn, cache_kt, cache_v2, page_table,
                                 diff_lambda[jl], subln_g[jl], lam_init=lam_init)
                o = jnp.pad(o.reshape(n_seq, width), ((0, m - n_seq), (0, 0)))
            xf = _res_proj(o, wo, jl, xf, tm=tm, tn=tn)
        if l == n_a - 1:
            xf, hn = _ffn(xf, norm_g[l, 2], ffn_w, (l, 1), kv_norm_g, tm=tm, norm=plain(BF16))
        elif l == depth - 1:
            (y,) = _ffn(xf, norm_g[l, 2], ffn_w, (l, 1), final_norm_g, tm=tm, emit_x=False, norm=plain(F32))
        else:
            (xf,) = _ffn(xf, norm_g[l, 2], ffn_w, (l, 1), final_norm_g, tm=tm)
    return y, states, k_out, v_out


def kernel(x_prompt, x_sample, state_ssm_re, state_ssm_im, cache_k, cache_v, page_table, norm_g, ffn_w_in, ffn_w_out, s5_a_re, s5_a_im, s5_log_dt, s5_b_re, s5_b_im, s5_c_re, s5_c_im, s5_d, s5_w_glu, kv_norm_g, w_kv, attn_w_q, attn_w_o, diff_lambda, subln_g, final_norm_g):
    b, seq, d = x_prompt.shape
    n_seq = x_sample.shape[0]
    n_a, _, n_groups, p = state_ssm_re.shape
    n_oct = n_groups // OCT_GROUPS
    n_phys, page, n_heads = cache_v.shape[:3]
    n_t = seq // SUBLANES
    n_pow = n_t.bit_length() - 1
    assert seq == SUBLANES * (1 << n_pow) and x_sample.shape[1] == 1 and n_seq <= SAMPLE_ROWS

    s5 = (s5_a_re, s5_a_im, s5_log_dt, s5_b_re, s5_b_im, s5_c_re, s5_c_im, s5_d)
    weights = _prep_weights(ffn_w_in, ffn_w_out, s5, s5_w_glu, w_kv, attn_w_q, attn_w_o, n_pow)
    gains = (norm_g, kv_norm_g, final_norm_g)

    h0_p = jnp.zeros((n_a, b, n_oct, 1, 2 * OCT_STATE), F32)
    y_p, st_p, kt_p, v_p = _trunk(x_prompt, True, weights, gains, h0_p, None, None, diff_lambda, subln_g)
    k_p = jnp.transpose(kt_p.reshape(b, n_heads, 2, QK_DIM, seq), (0, 4, 1, 2, 3))

    def unblock_prompt(st, half):
        t = st[:, :, 0, half * OCT_STATE:(half + 1) * OCT_STATE]
        return t.reshape(b, n_groups, p)

    re_p = jnp.stack([unblock_prompt(s, 0) for s in st_p])
    im_p = jnp.stack([unblock_prompt(s, 1) for s in st_p])

    pad = SAMPLE_ROWS - n_seq
    xs = jnp.pad(x_sample.reshape(n_seq, d), ((0, pad), (0, 0)))

    def block_sample(t):
        t = t.reshape(n_a, n_seq, n_oct, OCT_STATE).transpose(0, 2, 1, 3)
        return jnp.pad(t, ((0, 0), (0, 0), (0, pad), (0, 0)))

    h0_s = jnp.concatenate([block_sample(state_ssm_re), block_sample(state_ssm_im)], axis=-1)
    kv_cache = (jnp.transpose(cache_k, (0, 2, 3, 4, 1)).reshape(n_phys, -1, page),
                cache_v.reshape(n_phys, page * n_heads, V_DIM))
    y_s, st_s, k_s, v_s = _trunk(xs, False, weights, gains, h0_s, kv_cache, page_table, diff_lambda, subln_g)

    def unblock_sample(st, half):
        t = st[:, :n_seq, half * OCT_STATE:(half + 1) * OCT_STATE]
        return t.transpose(1, 0, 2).reshape(n_seq, n_groups, p)

    re_s = jnp.stack([unblock_sample(s, 0) for s in st_s])
    im_s = jnp.stack([unblock_sample(s, 1) for s in st_s])

    return (y_p.reshape(b, seq, d), y_s[:n_seq].reshape(n_seq, 1, d), re_p, im_p,
            k_p, v_p.reshape(b, seq, n_heads, V_DIM),
            re_s, im_s,
            k_s[:n_seq].reshape(n_seq, 1, n_heads, 2, QK_DIM), v_s[:n_seq].reshape(n_seq, 1, n_heads, V_DIM))
```

```python
import functools
import math

import jax
import jax.numpy as jnp
from jax import lax
from jax.experimental import pallas as pl
from jax.experimental.pallas import tpu as pltpu

F32 = jnp.float32
BF16 = jnp.bfloat16

GROUP_CH = 16
STATE_DIM = 64
QK_DIM = 64
V_DIM = 2 * QK_DIM
ROT_DIM = QK_DIM // 4
ROPE_THETA = 500000.0
NORM_EPS = 1e-6
NEG_INF = -1e30
Q_SCALE = QK_DIM ** -0.5
LOG2_E = 1.4426950408889634

LANES = 128
SUBLANES = 8
OCT_GROUPS = LANES // GROUP_CH
OCT_STATE = OCT_GROUPS * STATE_DIM
VMEM_LIMIT = 56 * 1024 * 1024

ROW_TILE = 512
COL_TILE = 2048
GLU_COL_TILE = 1024
FF_TILE = 512
FLASH_HEADS = 4
SAMPLE_ROWS = 16
PAGES_PER_STEP = 8


def _cparams(sem):
    return pltpu.CompilerParams(dimension_semantics=sem, vmem_limit_bytes=VMEM_LIMIT)


def _rms(x, g):
    return x * lax.rsqrt(jnp.mean(x * x, axis=-1, keepdims=True) + NORM_EPS) * g


def _sigmoid(x):
    return 1.0 / (1.0 + jnp.exp(-x))


def _ffn_kernel(x_ref, g_ref, gn_ref, wa_ref, wb_ref, wo_ref, *rest, n_main, has_tail, emit_x, emit_norm):
    tails, rest = (rest[:3], rest[3:]) if has_tail else ((), rest)
    outs, (h_scr, acc_scr) = rest[:-2], rest[-2:]
    j = pl.program_id(1)

    @pl.when(j == 0)
    def _():
        h_scr[...] = _rms(x_ref[...], g_ref[...]).astype(BF16)
        acc_scr[...] = jnp.zeros(acc_scr.shape, F32)

    def hidden(wa, wb, wo):
        h = h_scr[...]
        za = jnp.dot(h, wa[0, 0], preferred_element_type=F32)
        zb = jnp.dot(h, wb[0, 0], preferred_element_type=F32)
        act = (za * _sigmoid(za)) * zb
        return jnp.dot(act.astype(BF16), wo[0, 0], preferred_element_type=F32)

    acc_scr[...] += hidden(wa_ref, wb_ref, wo_ref)

    @pl.when(j == n_main - 1)
    def _():
        acc = acc_scr[...]
        if has_tail:
            acc = acc + hidden(*tails)
        xo = x_ref[...] + 0.5 * acc
        k = 0
        if emit_x:
            outs[k][...] = xo
            k += 1
        if emit_norm:
            outs[k][...] = _rms(xo, gn_ref[...]).astype(outs[k].dtype)


def _ffn(x, g, w, lk, gn, *, tm, emit_x=True, norm=None):
    m, d = x.shape
    l, k = lk
    w_in, w_out = w
    d_ff = w_out.shape[2]
    n_main = d_ff // FF_TILE
    main = n_main * FF_TILE
    tail = d_ff - main
    out_shape, out_specs = [], []
    if emit_x:
        out_shape.append(jax.ShapeDtypeStruct((m, d), F32))
        out_specs.append(pl.BlockSpec((tm, d), lambda i, j: (i, 0)))
    if norm is not None:
        out_shape.append(norm[0])
        out_specs.append(norm[1])
    row = pl.BlockSpec((1, d), lambda i, j: (0, 0))
    one, full = pl.Element(1), pl.Element(d)

    assert d_ff % LANES == 0 and FF_TILE % LANES == 0

    def offset(j, start, step):
        return (start // LANES + j * (step // LANES)) * LANES

    def cols(size, start, step):
        return pl.BlockSpec((one, one, full, pl.Element(size)),
                            lambda i, j: (l, k, 0, offset(j, start, step)))

    def rows(size, start, step):
        return pl.BlockSpec((one, one, pl.Element(size), full),
                            lambda i, j: (l, k, offset(j, start, step), 0))

    in_specs = [pl.BlockSpec((tm, d), lambda i, j: (i, 0)), row, row,
                cols(FF_TILE, 0, FF_TILE), cols(FF_TILE, d_ff, FF_TILE), rows(FF_TILE, 0, FF_TILE)]
    operands = [w_in, w_in, w_out]
    if tail:
        in_specs += [cols(tail, main, 0), cols(tail, d_ff + main, 0), rows(tail, main, 0)]
        operands += [w_in, w_in, w_out]
    return pl.pallas_call(
        functools.partial(_ffn_kernel, n_main=n_main, has_tail=bool(tail), emit_x=emit_x,
                          emit_norm=norm is not None),
        grid=(m // tm, n_main),
        in_specs=in_specs,
        out_specs=out_specs,
        out_shape=out_shape,
        scratch_shapes=[pltpu.VMEM((tm, d), BF16), pltpu.VMEM((tm, d), F32)],
        compiler_params=_cparams(("parallel", "arbitrary")),
        name="ffn",
    )(x, g.reshape(1, d), gn.reshape(1, d), *operands)


def _glu_kernel(a_ref, w1_ref, w2_ref, res_ref, o_ref):
    a = a_ref[...]
    z1 = jnp.dot(a, w1_ref[...], preferred_element_type=F32)
    z2 = jnp.dot(a, w2_ref[...], preferred_element_type=F32)
    o_ref[...] = res_ref[...] + z1 * _sigmoid(z2)


def _glu_proj(a, w, l, res, *, tm, tn):
    m, n = res.shape
    kdim = w.shape[1]
    nb = n // tn
    return pl.pallas_call(
        _glu_kernel,
        grid=(nb, m // tm),
        in_specs=[
            pl.BlockSpec((tm, kdim), lambda j, i: (i, 0)),
            pl.BlockSpec((None, kdim, tn), lambda j, i: (l, 0, j)),
            pl.BlockSpec((None, kdim, tn), lambda j, i: (l, 0, nb + j)),
            pl.BlockSpec((tm, tn), lambda j, i: (i, j)),
        ],
        out_specs=pl.BlockSpec((tm, tn), lambda j, i: (i, j)),
        out_shape=jax.ShapeDtypeStruct((m, n), F32),
        compiler_params=_cparams(("parallel", "arbitrary")),
        name="s5_glu",
    )(a, w, w, res)


def _res_proj_kernel(a_ref, w_ref, res_ref, o_ref):
    o_ref[...] = res_ref[...] + jnp.dot(a_ref[...], w_ref[...], preferred_element_type=F32)


def _res_proj(a, w, l, res, *, tm, tn):
    m, n = res.shape
    kdim = w.shape[1]
    return pl.pallas_call(
        _res_proj_kernel,
        grid=(n // tn, m // tm),
        in_specs=[
            pl.BlockSpec((tm, kdim), lambda j, i: (i, 0)),
            pl.BlockSpec((None, kdim, tn), lambda j, i: (l, 0, j)),
            pl.BlockSpec((tm, tn), lambda j, i: (i, j)),
        ],
        out_specs=pl.BlockSpec((tm, tn), lambda j, i: (i, j)),
        out_shape=jax.ShapeDtypeStruct((m, n), F32),
        compiler_params=_cparams(("parallel", "arbitrary")),
        name="attn_out_proj",
    )(a, w, res)


def _rope_proj_kernel(a_ref, w_ref, cos_ref, sa_ref, sb_ref, *outs, rope, scale, transposed):
    z = jnp.dot(a_ref[...], w_ref[...], preferred_element_type=F32)
    if rope:
        cos, sa, sb = cos_ref[...], sa_ref[...], sb_ref[...]
        cols = []
        for c in range(z.shape[1] // LANES):
            zc = z[:, c * LANES:(c + 1) * LANES]
            up = pltpu.roll(zc, LANES - ROT_DIM // 2, 1)
            dn = pltpu.roll(zc, ROT_DIM // 2, 1)
            cols.append(zc * cos + up * sa + dn * sb)
        z = jnp.concatenate(cols, axis=1)
    if scale != 1.0:
        z = z * scale
    zt = z.T if any(transposed) else None
    for o, t in zip(outs, transposed):
        o[...] = (zt if t else z).astype(o.dtype)


def _rope_proj(a, w, w_index, n, tables, *, tm, tn, rope, scale=1.0, outs=((BF16, False),), batch=1):
    m, kdim = a.shape
    n_tab = tables[0].shape[0] // tm
    tab_spec = pl.BlockSpec((tm, LANES), lambda j, i: (i % n_tab, 0))
    bpb = m // batch // tm
    out_specs, out_shape = [], []
    for dt, t in outs:
        if t:
            out_shape.append(jax.ShapeDtypeStruct((batch, n, m // batch), dt))
            out_specs.append(pl.BlockSpec((None, tn, tm), lambda j, i: (i // bpb, j, i % bpb)))
        else:
            out_shape.append(jax.ShapeDtypeStruct((m, n), dt))
            out_specs.append(pl.BlockSpec((tm, tn), lambda j, i: (i, j)))
    w_block = (None,) * (w.ndim - 2) + (kdim, tn)
    return pl.pallas_call(
        functools.partial(_rope_proj_kernel, rope=rope, scale=scale, transposed=tuple(t for _, t in outs)),
        grid=(n // tn, m // tm),
        in_specs=[
            pl.BlockSpec((tm, kdim), lambda j, i: (i, 0)),
            pl.BlockSpec(w_block, lambda j, i: w_index(j)),
            tab_spec, tab_spec, tab_spec,
        ],
        out_specs=out_specs,
        out_shape=out_shape,
        compiler_params=_cparams(("parallel", "arbitrary")),
        name="rope_proj" if rope else "plain_proj",
    )(a, w, *tables)


def _rope_tables(pos):
    inv = ROPE_THETA ** (-jnp.arange(0, ROT_DIM, 2, dtype=F32) / ROT_DIM)
    ang = pos.astype(F32)[:, None] * inv[None, :]
    cos, sin = jnp.cos(ang), jnp.sin(ang)
    half = ROT_DIM // 2
    n = pos.shape[0]
    ones = jnp.ones((n, QK_DIM - ROT_DIM), F32)
    zeros = jnp.zeros((n, QK_DIM - half), F32)
    cos64 = jnp.concatenate([cos, cos, ones], axis=1)
    sa64 = jnp.concatenate([-sin, zeros], axis=1)
    sb64 = jnp.concatenate([jnp.zeros((n, half), F32), sin, jnp.zeros((n, QK_DIM - ROT_DIM), F32)], axis=1)
    rep = LANES // QK_DIM
    return tuple(jnp.tile(t, (1, rep)) for t in (cos64, sa64, sb64))


def _s5_step(ar, ai, hr, hi, xr, xi):
    return (ar * hr + xr) - ai * hi, (ar * hi + xi) + ai * hr


def _s5_scan_kernel(u_ref, h0_ref, bw_ref, cw_ref, a_ref, d_ref, g_ref, ht_ref,
                    x_scr, f_scr, hs_scr, *, seq, rc):
    n_t = seq // SUBLANES
    chunks = [(s, t0) for s in range(SUBLANES) for t0 in range(0, n_t, rc)]
    p = OCT_STATE
    nc = p // LANES
    bw = bw_ref[...]

    def lanes(c):
        return slice(c * LANES, (c + 1) * LANES)

    for s, t0 in chunks:
        x = jnp.dot(u_ref[pl.ds(s * n_t + t0, rc), :].astype(BF16), bw, preferred_element_type=F32)
        for c in range(2 * nc):
            x_scr[c, pl.ds(t0 * SUBLANES + s, rc, stride=SUBLANES), :] = x[:, lanes(c)]

    a = a_ref[...]
    ar = [jnp.broadcast_to(a[0:1, lanes(c)], (SUBLANES, LANES)) for c in range(nc)]
    ai = [jnp.broadcast_to(a[1:2, lanes(c)], (SUBLANES, LANES)) for c in range(nc)]

    def make_step(store):
        def step(t, carry):
            r0 = pl.multiple_of(t * SUBLANES, SUBLANES)
            new = []
            for c in range(nc):
                nr, ni = _s5_step(ar[c], ai[c], carry[c][0], carry[c][1],
                                  x_scr[c, pl.ds(r0, SUBLANES), :], x_scr[nc + c, pl.ds(r0, SUBLANES), :])
                if store:
                    x_scr[c, pl.ds(r0, SUBLANES), :] = nr
                    x_scr[nc + c, pl.ds(r0, SUBLANES), :] = ni
                new.append((nr, ni))
            return tuple(new)
        return step

    zero = jnp.zeros((SUBLANES, LANES), F32)
    fin = lax.fori_loop(0, n_t, make_step(False), tuple((zero, zero) for _ in range(nc)), unroll=8)
    for c in range(nc):
        f_scr[:, lanes(c)] = fin[c][0]
        f_scr[:, lanes(nc + c)] = fin[c][1]

    atr, ati = a[2:3], a[3:4]
    pr, pi = h0_ref[:, 0:p], h0_ref[:, p:2 * p]
    hs_scr[0:1, 0:p] = pr
    hs_scr[0:1, p:2 * p] = pi
    for s in range(1, SUBLANES):
        pr, pi = _s5_step(atr, ati, pr, pi, f_scr[s - 1:s, 0:p], f_scr[s - 1:s, p:2 * p])
        hs_scr[s:s + 1, 0:p] = pr
        hs_scr[s:s + 1, p:2 * p] = pi

    start = tuple((hs_scr[:, lanes(c)], hs_scr[:, lanes(nc + c)]) for c in range(nc))
    fin = lax.fori_loop(0, n_t, make_step(True), start, unroll=8)
    for c in range(nc):
        ht_ref[:, lanes(c)] = fin[c][0][SUBLANES - 1:SUBLANES]
        ht_ref[:, lanes(nc + c)] = fin[c][1][SUBLANES - 1:SUBLANES]

    cw = cw_ref[...]
    d = d_ref[...]

    for s, t0 in chunks:
        rows = pl.ds(s * n_t + t0, rc)
        h = jnp.concatenate([x_scr[c, pl.ds(t0 * SUBLANES + s, rc, stride=SUBLANES), :].astype(BF16)
                             for c in range(2 * nc)], axis=1)
        y = jnp.dot(h, cw, preferred_element_type=F32) + d * u_ref[rows, :]
        g_ref[rows, :] = jax.nn.gelu(y).astype(BF16)


def _s5_scan(u, h0, prm, *, rc):
    b, seq, d = u.shape
    n_oct = d // LANES
    p2 = 2 * OCT_STATE
    bw, cw, arows, drow = prm
    return pl.pallas_call(
        functools.partial(_s5_scan_kernel, seq=seq, rc=rc),
        grid=(b, n_oct),
        in_specs=[
            pl.BlockSpec((None, seq, LANES), lambda i, o: (i, 0, o)),
            pl.BlockSpec((None, None, 1, p2), lambda i, o: (i, o, 0, 0)),
            pl.BlockSpec((None, LANES, p2), lambda i, o: (o, 0, 0)),
            pl.BlockSpec((None, p2, LANES), lambda i, o: (o, 0, 0)),
            pl.BlockSpec((None, 4, OCT_STATE), lambda i, o: (o, 0, 0)),
            pl.BlockSpec((None, 1, LANES), lambda i, o: (o, 0, 0)),
        ],
        out_specs=[
            pl.BlockSpec((None, seq, LANES), lambda i, o: (i, 0, o)),
            pl.BlockSpec((None, None, 1, p2), lambda i, o: (i, o, 0, 0)),
        ],
        out_shape=[
            jax.ShapeDtypeStruct((b, seq, d), BF16),
            jax.ShapeDtypeStruct((b, n_oct, 1, p2), F32),
        ],
        scratch_shapes=[
            pltpu.VMEM((p2 // LANES, seq, LANES), F32),
            pltpu.VMEM((SUBLANES, p2), F32),
            pltpu.VMEM((SUBLANES, p2), F32),
        ],
        compiler_params=_cparams(("parallel", "arbitrary")),
        name="s5_scan",
    )(u, h0, bw, cw, arows, drow)


def _s5_single_kernel(u_ref, h0_ref, bw_ref, cw_ref, a_ref, d_ref, g_ref, hn_ref):
    p = OCT_STATE
    u = u_ref[...]
    x = jnp.dot(u.astype(BF16), bw_ref[...], preferred_element_type=F32)
    a = a_ref[...]
    hr, hi = _s5_step(a[0:1], a[1:2], h0_ref[:, 0:p], h0_ref[:, p:2 * p], x[:, 0:p], x[:, p:2 * p])
    hn_ref[:, 0:p] = hr
    hn_ref[:, p:2 * p] = hi
    y = jnp.dot(hn_ref[...].astype(BF16), cw_ref[...], preferred_element_type=F32) + d_ref[...] * u
    g_ref[...] = jax.nn.gelu(y).astype(BF16)


def _s5_single(u, h0, prm):
    m, d = u.shape
    n_oct = d // LANES
    p2 = 2 * OCT_STATE
    bw, cw, arows, drow = prm
    return pl.pallas_call(
        _s5_single_kernel,
        grid=(n_oct,),
        in_specs=[
            pl.BlockSpec((m, LANES), lambda o: (0, o)),
            pl.BlockSpec((None, m, p2), lambda o: (o, 0, 0)),
            pl.BlockSpec((None, LANES, p2), lambda o: (o, 0, 0)),
            pl.BlockSpec((None, p2, LANES), lambda o: (o, 0, 0)),
            pl.BlockSpec((None, 4, OCT_STATE), lambda o: (o, 0, 0)),
            pl.BlockSpec((None, 1, LANES), lambda o: (o, 0, 0)),
        ],
        out_specs=[
            pl.BlockSpec((m, LANES), lambda o: (0, o)),
            pl.BlockSpec((None, m, p2), lambda o: (o, 0, 0)),
        ],
        out_shape=[
            jax.ShapeDtypeStruct((m, d), BF16),
            jax.ShapeDtypeStruct((n_oct, m, p2), F32),
        ],
        compiler_params=_cparams(("parallel",)),
        name="s5_single",
    )(u, h0, bw, cw, arows, drow)


def _s5_params(a_re, a_im, log_dt, b_re, b_im, c_re, c_im, d, n_pow):
    g = a_re.shape[0]
    n_oct = g // OCT_GROUPS
    ar, ai = a_re.astype(F32), a_im.astype(F32)
    dt = jnp.exp(log_dt.astype(F32))[:, None]
    mag = jnp.exp(dt * ar)
    ab_re = mag * jnp.cos(dt * ai)
    ab_im = mag * jnp.sin(dt * ai)
    den = ar * ar + ai * ai
    n_re = ab_re - 1.0
    n_im = ab_im
    f_re = (n_re * ar + n_im * ai) / den
    f_im = (n_im * ar - n_re * ai) / den
    br, bi = b_re.astype(F32), b_im.astype(F32)
    bb_re = f_re[..., None] * br - f_im[..., None] * bi
    bb_im = f_re[..., None] * bi + f_im[..., None] * br
    eye = jnp.eye(OCT_GROUPS, dtype=F32)

    def in_blocks(bb):
        t = bb.reshape(n_oct, OCT_GROUPS, STATE_DIM, GROUP_CH)
        t = jnp.einsum('ogpc,gh->ogchp', t, eye)
        return t.reshape(n_oct, LANES, OCT_STATE)

    def out_blocks(cc):
        t = cc.astype(F32).reshape(n_oct, OCT_GROUPS, GROUP_CH, STATE_DIM)
        t = jnp.einsum('ogcp,gh->ogphc', t, eye)
        return t.reshape(n_oct, OCT_STATE, LANES)

    bw = jnp.concatenate([in_blocks(bb_re), in_blocks(bb_im)], axis=2).astype(BF16)
    cw = jnp.concatenate([out_blocks(c_re), -out_blocks(c_im)], axis=1).astype(BF16)
    pw_re, pw_im = ab_re, ab_im
    for _ in range(n_pow):
        pw_re, pw_im = pw_re * pw_re - pw_im * pw_im, 2.0 * pw_re * pw_im
    arows = jnp.stack([t.reshape(n_oct, OCT_STATE) for t in (ab_re, ab_im, pw_re, pw_im)], axis=1)
    drow = d.astype(F32).reshape(n_oct, 1, LANES)
    return bw, cw, arows, drow


def _lambda(lp, lam_init):
    a = jnp.sum(lp[0:1] * lp[1:2], axis=1, keepdims=True)
    b = jnp.sum(lp[2:3] * lp[3:4], axis=1, keepdims=True)
    return jnp.exp(a) - jnp.exp(b) + lam_init


def _flash_kernel(qtab, ktab, qt_ref, k_ref, vt_ref, lp_ref, sg_ref, o_ref, m_scr, l_scr, a_scr, *, lam_init):
    step = pl.program_id(2)
    qi, ki = qtab[step], ktab[step]
    heads = k_ref.shape[1] // LANES

    @pl.when(ki == 0)
    def _():
        m_scr[...] = jnp.full(m_scr.shape, NEG_INF, F32)
        l_scr[...] = jnp.zeros(l_scr.shape, F32)
        a_scr[...] = jnp.zeros(a_scr.shape, F32)

    def block(diagonal):
        tk, tq = k_ref.shape[0], qt_ref.shape[1]
        feat = lax.broadcasted_iota(jnp.int32, (LANES, tq), 0)
        if diagonal:
            keep = (lax.broadcasted_iota(jnp.int32, (tk, tq), 0)
                    <= lax.broadcasted_iota(jnp.int32, (tk, tq), 1))
        scores = []
        for h in range(heads):
            qt = qt_ref[h * LANES:(h + 1) * LANES, :]
            k = k_ref[:, h * LANES:(h + 1) * LANES]
            zero = jnp.zeros_like(qt)
            for qm in (jnp.where(feat < QK_DIM, qt, zero), jnp.where(feat >= QK_DIM, qt, zero)):
                scores.append(jnp.dot(k, qm, preferred_element_type=F32))
        probs, alphas = [], []
        for c, st in enumerate(scores):
            if diagonal:
                st = jnp.where(keep, st, NEG_INF)
            m_old = m_scr[c:c + 1, :]
            m_new = jnp.maximum(m_old, jnp.max(st, axis=0, keepdims=True))
            alpha = jnp.exp2(m_old - m_new)
            pt = jnp.exp2(st - m_new)
            l_scr[c:c + 1, :] = alpha * l_scr[c:c + 1, :] + jnp.sum(pt, axis=0, keepdims=True)
            m_scr[c:c + 1, :] = m_new
            probs.append(pt.astype(BF16))
            alphas.append(alpha)
        for c, (pt, alpha) in enumerate(zip(probs, alphas)):
            h = c // 2
            rows = slice(c * V_DIM, (c + 1) * V_DIM)
            vt = vt_ref[h * LANES:(h + 1) * LANES, :]
            a_scr[rows, :] = alpha * a_scr[rows, :] + jnp.dot(vt, pt, preferred_element_type=F32)

    @pl.when(ki < qi)
    def _():
        block(False)

    @pl.when(ki == qi)
    def _():
        block(True)
        lam = _lambda(lp_ref[...], lam_init)
        for h in range(heads):
            c = 2 * h
            o1 = a_scr[c * V_DIM:(c + 1) * V_DIM, :] / l_scr[c:c + 1, :]
            o2 = a_scr[(c + 1) * V_DIM:(c + 2) * V_DIM, :] / l_scr[c + 1:c + 2, :]
            ot = o1 - lam * o2
            ms = jnp.mean(ot * ot, axis=0, keepdims=True)
            ot = ot * lax.rsqrt(ms + NORM_EPS) * sg_ref[...] * (1.0 - lam_init)
            o_ref[:, h * LANES:(h + 1) * LANES] = ot.T.astype(o_ref.dtype)


def _flash(qt, k, vt, lp, sg, *, tq, lam_init):
    b, seq, width = k.shape
    n_heads = width // LANES
    hp = FLASH_HEADS if n_heads % FLASH_HEADS == 0 else 1
    hw = hp * LANES
    nq = seq // tq
    pairs = [(i, j) for i in range(nq) for j in range(i + 1)]
    qtab = jnp.asarray([p[0] for p in pairs], jnp.int32)
    ktab = jnp.asarray([p[1] for p in pairs], jnp.int32)
    grid_spec = pltpu.PrefetchScalarGridSpec(
        num_scalar_prefetch=2,
        grid=(b, n_heads // hp, len(pairs)),
        in_specs=[
            pl.BlockSpec((None, hw, tq), lambda i, h, s, qt_, kt_: (i, h, qt_[s])),
            pl.BlockSpec((None, tq, hw), lambda i, h, s, qt_, kt_: (i, kt_[s], h)),
            pl.BlockSpec((None, hw, tq), lambda i, h, s, qt_, kt_: (i, h, kt_[s])),
            pl.BlockSpec((4, QK_DIM), lambda i, h, s, qt_, kt_: (0, 0)),
            pl.BlockSpec((V_DIM, 1), lambda i, h, s, qt_, kt_: (0, 0)),
        ],
        out_specs=pl.BlockSpec((None, tq, hw), lambda i, h, s, qt_, kt_: (i, qt_[s], h)),
        scratch_shapes=[
            pltpu.VMEM((2 * hp, tq), F32), pltpu.VMEM((2 * hp, tq), F32),
            pltpu.VMEM((2 * hp * V_DIM, tq), F32),
        ],
    )
    return pl.pallas_call(
        functools.partial(_flash_kernel, lam_init=lam_init),
        grid_spec=grid_spec,
        out_shape=jax.ShapeDtypeStruct((b, seq, width), BF16),
        compiler_params=_cparams(("parallel", "parallel", "arbitrary")),
        name="diff_flash",
    )(qtab, ktab, qt, k, vt, lp, sg.reshape(V_DIM, 1))


def _decode_kernel(pt_ref, qcol_ref, *refs, pages, n_heads, lam_init):
    k_refs, v_refs = refs[:pages], refs[pages:2 * pages]
    kn_ref, vn_ref, rep_ref, lp_ref, sg_ref, o_ref, s_scr, m_scr, l_scr, acc_scr = refs[2 * pages:]
    j = pl.program_id(1)
    rows = 2 * n_heads
    n_tok = kn_ref.shape[1]

    @pl.when(j == 0)
    def _():
        m_scr[...] = jnp.full(m_scr.shape, NEG_INF, F32)
        l_scr[...] = jnp.zeros(l_scr.shape, F32)
        acc_scr[...] = jnp.zeros(acc_scr.shape, F32)

    rowi = lax.broadcasted_iota(jnp.int32, (rows, n_tok * n_heads), 0)
    coli = lax.broadcasted_iota(jnp.int32, (rows, n_tok * n_heads), 1)
    head_sel = (coli & (n_heads - 1)) == (rowi >> 1)

    def page(kt_ref, v_ref, n_valid):
        prod = kt_ref[...] * qcol_ref[...]
        for c in range(rows):
            s_scr[c:c + 1, :] = jnp.sum(prod[c * QK_DIM:(c + 1) * QK_DIM, :], axis=0, keepdims=True)
        s = s_scr[...]
        if n_valid is not None:
            s = jnp.where(lax.broadcasted_iota(jnp.int32, s.shape, 1) < n_valid, s, NEG_INF)
        m_old = m_scr[...]
        m_new = jnp.maximum(m_old, jnp.max(s, axis=1, keepdims=True))
        alpha = jnp.exp2(m_old - m_new)
        p = jnp.exp2(s - m_new)
        l_scr[...] = alpha * l_scr[...] + jnp.sum(p, axis=1, keepdims=True)
        spread = jnp.dot(p.astype(BF16), rep_ref[...], preferred_element_type=F32)
        spread = jnp.where(head_sel, spread, 0.0).astype(BF16)
        pv = jnp.dot(spread, v_ref[...].astype(BF16), preferred_element_type=F32)
        acc_scr[...] = alpha * acc_scr[...] + pv
        m_scr[...] = m_new

    for i in range(pages):
        page(k_refs[i], v_refs[i], None)

    @pl.when(j == pl.num_programs(1) - 1)
    def _():
        page(kn_ref, vn_ref, 1)
        lam = _lambda(lp_ref[...], lam_init)
        s_scr[...] = acc_scr[...] / l_scr[...]
        o = s_scr[pl.ds(0, n_heads, stride=2), :] - lam * s_scr[pl.ds(1, n_heads, stride=2), :]
        o_ref[...] = (_rms(o, sg_ref[...]) * (1.0 - lam_init)).astype(o_ref.dtype)


def _decode_attn(qcol, kn, vn, cache_kt, cache_v2, page_table, lp, sg, *, lam_init):
    n_seq, n_pages = page_table.shape
    _, width, page = cache_kt.shape
    n_heads = width // V_DIM
    assert n_heads & (n_heads - 1) == 0 and page == LANES
    pages = PAGES_PER_STEP if n_pages % PAGES_PER_STEP == 0 else 1
    seq_k = pl.BlockSpec((None, width, page), lambda b, j, pt: (b, 0, 0))
    seq_v = pl.BlockSpec((None, page * n_heads, V_DIM), lambda b, j, pt: (b, 0, 0))
    rep = (jnp.arange(page * n_heads)[None, :] // n_heads == jnp.arange(page)[:, None]).astype(BF16)

    def page_spec(shape, i):
        return pl.BlockSpec((None,) + shape, lambda b, j, pt: (pt[b, j * pages + i], 0, 0))

    grid_spec = pltpu.PrefetchScalarGridSpec(
        num_scalar_prefetch=1,
        grid=(n_seq, n_pages // pages),
        in_specs=[seq_k]
        + [page_spec((width, page), i) for i in range(pages)]
        + [page_spec((page * n_heads, V_DIM), i) for i in range(pages)]
        + [seq_k, seq_v,
           pl.BlockSpec((page, page * n_heads), lambda b, j, pt: (0, 0)),
           pl.BlockSpec((4, QK_DIM), lambda b, j, pt: (0, 0)),
           pl.BlockSpec((1, V_DIM), lambda b, j, pt: (0, 0))],
        out_specs=pl.BlockSpec((None, n_heads, V_DIM), lambda b, j, pt: (b, 0, 0)),
        scratch_shapes=[
            pltpu.VMEM((2 * n_heads, page), F32),
            pltpu.VMEM((2 * n_heads, 1), F32), pltpu.VMEM((2 * n_heads, 1), F32),
            pltpu.VMEM((2 * n_heads, V_DIM), F32),
        ],
    )
    return pl.pallas_call(
        functools.partial(_decode_kernel, pages=pages, n_heads=n_heads, lam_init=lam_init),
        grid_spec=grid_spec,
        out_shape=jax.ShapeDtypeStruct((n_seq, n_heads, V_DIM), BF16),
        compiler_params=_cparams(("parallel", "arbitrary")),
        name="diff_decode",
    )(page_table, qcol, *([cache_kt] * pages), *([cache_v2] * pages), kn, vn, rep, lp, sg.reshape(1, V_DIM))


def _prep_weights(ffn_w_in, ffn_w_out, s5, s5_w_glu, w_kv, attn_w_q, attn_w_o, n_pow):
    assert ffn_w_out.shape[2] >= FF_TILE
    s5_prm = [_s5_params(*(t[l] for t in s5), n_pow) for l in range(s5_w_glu.shape[0])]
    return ((ffn_w_in.astype(BF16), ffn_w_out.astype(BF16)), s5_prm, s5_w_glu.astype(BF16),
            w_kv.astype(BF16), attn_w_q.astype(BF16), attn_w_o.astype(BF16))


def _trunk(x, prompt, weights, gains, h0, kv_cache, page_table, diff_lambda, subln_g):
    ffn_w, s5_prm, glu_w, wkv, wq, wo = weights
    norm_g, kv_norm_g, final_norm_g = gains
    depth = norm_g.shape[0]
    n_a = len(s5_prm)
    if prompt:
        b, seq, d = x.shape
        m = b * seq
        tm = tq = min(ROW_TILE, seq // SUBLANES)
        pos = jnp.arange(seq)
    else:
        m, d = x.shape
        b, tm = 1, m
        pos = jnp.full((m,), page_table.shape[1] * kv_cache[0].shape[2], jnp.int32)
    tn = min(COL_TILE, d)
    qk_blocks = wq.shape[2] // tn
    tables = _rope_tables(pos)
    xf = x.reshape(m, d)
    plain = lambda dt: (jax.ShapeDtypeStruct((m, d), dt), pl.BlockSpec((tm, d), lambda i, j: (i, 0)))
    proj = functools.partial(_rope_proj, tables=tables, tm=tm, tn=tn, batch=b)
    states, k_out, v_out, k_att, v_att, hn = [], None, None, None, None, None
    for l in range(depth):
        if l == n_a:
            k_outs = ((BF16, False), (F32, True)) if prompt else ((F32, False),)
            v_outs = ((F32, False), (BF16, True)) if prompt else ((F32, False),)
            k_res = proj(hn, wkv, lambda j: (0, j), wq.shape[2], rope=True, outs=k_outs)
            v_res = proj(hn, wkv, lambda j: (0, qk_blocks + j), wkv.shape[1] - wq.shape[2], rope=False, outs=v_outs)
            k_att, k_out = k_res if prompt else (None, k_res[0])
            v_out, v_att = v_res if prompt else (v_res[0], None)
        xf, hn = _ffn(xf, norm_g[l, 0], ffn_w, (l, 0), norm_g[l, 1], tm=tm, norm=plain(F32 if l < n_a else BF16))
        if l < n_a:
            if prompt:
                g, st = _s5_scan(hn.reshape(b, seq, d), h0[l], s5_prm[l], rc=tm)
                g = g.reshape(m, d)
            else:
                g, st = _s5_single(hn, h0[l], s5_prm[l])
            states.append(st)
            xf = _glu_proj(g, glu_w, l, xf, tm=tm, tn=min(GLU_COL_TILE, d))
        else:
            jl = l - n_a
            lam_init = 0.8 - 0.6 * math.exp(-0.3 * l)
            q_outs = ((BF16, True),) if prompt else ((BF16, False),)
            q = proj(hn, wq, lambda j, jl=jl: (jl, 0, j), wq.shape[2], rope=True,
                     scale=Q_SCALE * LOG2_E, outs=q_outs)[0]
            if prompt:
                o = _flash(q, k_att.reshape(b, seq, -1), v_att, diff_lambda[jl], subln_g[jl],
                           tq=tq, lam_init=lam_init).reshape(m, -1)
            else:
                n_seq = page_table.shape[0]
                cache_kt, cache_v2 = kv_cache
                width, page = cache_kt.shape[1:]
                n_heads = width // V_DIM
                qcol = jnp.broadcast_to(q[:n_seq].astype(F32)[:, :, None], (n_seq, width, page))
                kn = jnp.zeros((n_seq, width, page), F32).at[:, :, 0].set(k_out[:n_seq])
                vn = jnp.zeros((n_seq, page * n_heads, V_DIM), F32).at[:, :n_heads, :].set(
                    v_out[:n_seq].reshape(n_seq, n_heads, V_DIM))
                o = _decode_attn(qcol, kn, vn, cache_kt, cache_v2, page_table,
                                 diff_lambda[jl], subln_g[jl], lam_init=lam_init)
                o = jnp.pad(o.reshape(n_seq, width), ((0, m - n_seq), (0, 0)))
            xf = _res_proj(o, wo, jl, xf, tm=tm, tn=tn)
        if l == n_a - 1:
            xf, hn = _ffn(xf, norm_g[l, 2], ffn_w, (l, 1), kv_norm_g, tm=tm, norm=plain(BF16))
        elif l == depth - 1:
            (y,) = _ffn(xf, norm_g[l, 2], ffn_w, (l, 1), final_norm_g, tm=tm, emit_x=False, norm=plain(F32))
        else:
            (xf,) = _ffn(xf, norm_g[l, 2], ffn_w, (l, 1), final_norm_g, tm=tm)
    return y, states, k_out, v_out


def kernel(x_prompt, x_sample, state_ssm_re, state_ssm_im, cache_k, cache_v, page_table, norm_g, ffn_w_in, ffn_w_out, s5_a_re, s5_a_im, s5_log_dt, s5_b_re, s5_b_im, s5_c_re, s5_c_im, s5_d, s5_w_glu, kv_norm_g, w_kv, attn_w_q, attn_w_o, diff_lambda, subln_g, final_norm_g):
    b, seq, d = x_prompt.shape
    n_seq = x_sample.shape[0]
    n_a, _, n_groups, p = state_ssm_re.shape
    n_oct = n_groups // OCT_GROUPS
    n_phys, page, n_heads = cache_v.shape[:3]
    n_t = seq // SUBLANES
    n_pow = n_t.bit_length() - 1
    assert seq == SUBLANES * (1 << n_pow) and x_sample.shape[1] == 1 and n_seq <= SAMPLE_ROWS

    s5 = (s5_a_re, s5_a_im, s5_log_dt, s5_b_re, s5_b_im, s5_c_re, s5_c_im, s5_d)
    weights = _prep_weights(ffn_w_in, ffn_w_out, s5, s5_w_glu, w_kv, attn_w_q, attn_w_o, n_pow)
    gains = (norm_g, kv_norm_g, final_norm_g)

    h0_p = jnp.zeros((n_a, b, n_oct, 1, 2 * OCT_STATE), F32)
    y_p, st_p, kt_p, v_p = _trunk(x_prompt, True, weights, gains, h0_p, None, None, diff_lambda, subln_g)
    k_p = jnp.transpose(kt_p.reshape(b, n_heads, 2, QK_DIM, seq), (0, 4, 1, 2, 3))

    def unblock_prompt(st, half):
        t = st[:, :, 0, half * OCT_STATE:(half + 1) * OCT_STATE]
        return t.reshape(b, n_groups, p)

    re_p = jnp.stack([unblock_prompt(s, 0) for s in st_p])
    im_p = jnp.stack([unblock_prompt(s, 1) for s in st_p])

    pad = SAMPLE_ROWS - n_seq
    xs = jnp.pad(x_sample.reshape(n_seq, d), ((0, pad), (0, 0)))

    def block_sample(t):
        t = t.reshape(n_a, n_seq, n_oct, OCT_STATE).transpose(0, 2, 1, 3)
        return jnp.pad(t, ((0, 0), (0, 0), (0, pad), (0, 0)))

    h0_s = jnp.concatenate([block_sample(state_ssm_re), block_sample(state_ssm_im)], axis=-1)
    kv_cache = (jnp.transpose(cache_k, (0, 2, 3, 4, 1)).reshape(n_phys, -1, page),
                cache_v.reshape(n_phys, page * n_heads, V_DIM))
    y_s, st_s, k_s, v_s = _trunk(xs, False, weights, gains, h0_s, kv_cache, page_table, diff_lambda, subln_g)

    def unblock_sample(st, half):
        t = st[:, :n_seq, half * OCT_STATE:(half + 1) * OCT_STATE]
        return t.transpose(1, 0, 2).reshape(n_seq, n_groups, p)

    re_s = jnp.stack([unblock_sample(s, 0) for s in st_s])
    im_s = jnp.stack([unblock_sample(s, 1) for s in st_s])

    return (y_p.reshape(b, seq, d), y_s[:n_seq].reshape(n_seq, 1, d), re_p, im_p,
            k_p, v_p.reshape(b, seq, n_heads, V_DIM),
            re_s, im_s,
            k_s[:n_seq].reshape(n_seq, 1, n_heads, 2, QK_DIM), v_s[:n_seq].reshape(n_seq, 1, n_heads, V_DIM))
```

```python
import functools
import math

import jax
import jax.numpy as jnp
from jax import lax
from jax.experimental import pallas as pl
from jax.experimental.pallas import tpu as pltpu

F32 = jnp.float32
BF16 = jnp.bfloat16

GROUP_CH = 16
STATE_DIM = 64
QK_DIM = 64
V_DIM = 2 * QK_DIM
ROT_DIM = QK_DIM // 4
ROPE_THETA = 500000.0
NORM_EPS = 1e-6
NEG_INF = -1e30
Q_SCALE = QK_DIM ** -0.5
LOG2_E = 1.4426950408889634

LANES = 128
SUBLANES = 8
OCT_GROUPS = LANES // GROUP_CH
OCT_STATE = OCT_GROUPS * STATE_DIM
VMEM_LIMIT = 56 * 1024 * 1024

ROW_TILE = 512
COL_TILE = 2048
GLU_COL_TILE = 1024
FF_TILE = 512
FLASH_HEADS = 4
SAMPLE_ROWS = 16
PAGES_PER_STEP = 4


def _cparams(sem):
    return pltpu.CompilerParams(dimension_semantics=sem, vmem_limit_bytes=VMEM_LIMIT)


def _rms(x, g):
    return x * lax.rsqrt(jnp.mean(x * x, axis=-1, keepdims=True) + NORM_EPS) * g


def _sigmoid(x):
    return 1.0 / (1.0 + jnp.exp(-x))


def _ffn_kernel(x_ref, g_ref, gn_ref, wa_ref, wb_ref, wo_ref, *rest, n_main, has_tail, emit_x, emit_norm):
    tails, rest = (rest[:3], rest[3:]) if has_tail else ((), rest)
    outs, (h_scr, acc_scr) = rest[:-2], rest[-2:]
    j = pl.program_id(1)

    @pl.when(j == 0)
    def _():
        h_scr[...] = _rms(x_ref[...], g_ref[...]).astype(BF16)
        acc_scr[...] = jnp.zeros(acc_scr.shape, F32)

    def hidden(wa, wb, wo):
        h = h_scr[...]
        za = jnp.dot(h, wa[0, 0], preferred_element_type=F32)
        zb = jnp.dot(h, wb[0, 0], preferred_element_type=F32)
        act = (za * _sigmoid(za)) * zb
        return jnp.dot(act.astype(BF16), wo[0, 0], preferred_element_type=F32)

    acc_scr[...] += hidden(wa_ref, wb_ref, wo_ref)

    @pl.when(j == n_main - 1)
    def _():
        acc = acc_scr[...]
        if has_tail:
            acc = acc + hidden(*tails)
        xo = x_ref[...] + 0.5 * acc
        k = 0
        if emit_x:
            outs[k][...] = xo
            k += 1
        if emit_norm:
            outs[k][...] = _rms(xo, gn_ref[...]).astype(outs[k].dtype)


def _ffn(x, g, w, lk, gn, *, tm, emit_x=True, norm=None):
    m, d = x.shape
    l, k = lk
    w_in, w_out = w
    d_ff = w_out.shape[2]
    n_main = d_ff // FF_TILE
    main = n_main * FF_TILE
    tail = d_ff - main
    out_shape, out_specs = [], []
    if emit_x:
        out_shape.append(jax.ShapeDtypeStruct((m, d), F32))
        out_specs.append(pl.BlockSpec((tm, d), lambda i, j: (i, 0)))
    if norm is not None:
        out_shape.append(norm[0])
        out_specs.append(norm[1])
    row = pl.BlockSpec((1, d), lambda i, j: (0, 0))
    one, full = pl.Element(1), pl.Element(d)

    assert d_ff % LANES == 0 and FF_TILE % LANES == 0

    def offset(j, start, step):
        return (start // LANES + j * (step // LANES)) * LANES

    def cols(size, start, step):
        return pl.BlockSpec((one, one, full, pl.Element(size)),
                            lambda i, j: (l, k, 0, offset(j, start, step)))

    def rows(size, start, step):
        return pl.BlockSpec((one, one, pl.Element(size), full),
                            lambda i, j: (l, k, offset(j, start, step), 0))

    in_specs = [pl.BlockSpec((tm, d), lambda i, j: (i, 0)), row, row,
                cols(FF_TILE, 0, FF_TILE), cols(FF_TILE, d_ff, FF_TILE), rows(FF_TILE, 0, FF_TILE)]
    operands = [w_in, w_in, w_out]
    if tail:
        in_specs += [cols(tail, main, 0), cols(tail, d_ff + main, 0), rows(tail, main, 0)]
        operands += [w_in, w_in, w_out]
    return pl.pallas_call(
        functools.partial(_ffn_kernel, n_main=n_main, has_tail=bool(tail), emit_x=emit_x,
                          emit_norm=norm is not None),
        grid=(m // tm, n_main),
        in_specs=in_specs,
        out_specs=out_specs,
        out_shape=out_shape,
        scratch_shapes=[pltpu.VMEM((tm, d), BF16), pltpu.VMEM((tm, d), F32)],
        compiler_params=_cparams(("parallel", "arbitrary")),
        name="ffn",
    )(x, g.reshape(1, d), gn.reshape(1, d), *operands)


def _glu_kernel(a_ref, w1_ref, w2_ref, res_ref, o_ref):
    a = a_ref[...]
    z1 = jnp.dot(a, w1_ref[...], preferred_element_type=F32)
    z2 = jnp.dot(a, w2_ref[...], preferred_element_type=F32)
    o_ref[...] = res_ref[...] + z1 * _sigmoid(z2)


def _glu_proj(a, w, l, res, *, tm, tn):
    m, n = res.shape
    kdim = w.shape[1]
    nb = n // tn
    return pl.pallas_call(
        _glu_kernel,
        grid=(nb, m // tm),
        in_specs=[
            pl.BlockSpec((tm, kdim), lambda j, i: (i, 0)),
            pl.BlockSpec((None, kdim, tn), lambda j, i: (l, 0, j)),
            pl.BlockSpec((None, kdim, tn), lambda j, i: (l, 0, nb + j)),
            pl.BlockSpec((tm, tn), lambda j, i: (i, j)),
        ],
        out_specs=pl.BlockSpec((tm, tn), lambda j, i: (i, j)),
        out_shape=jax.ShapeDtypeStruct((m, n), F32),
        compiler_params=_cparams(("parallel", "arbitrary")),
        name="s5_glu",
    )(a, w, w, res)


def _res_proj_kernel(a_ref, w_ref, res_ref, o_ref):
    o_ref[...] = res_ref[...] + jnp.dot(a_ref[...], w_ref[...], preferred_element_type=F32)


def _res_proj(a, w, l, res, *, tm, tn):
    m, n = res.shape
    kdim = w.shape[1]
    return pl.pallas_call(
        _res_proj_kernel,
        grid=(n // tn, m // tm),
        in_specs=[
            pl.BlockSpec((tm, kdim), lambda j, i: (i, 0)),
            pl.BlockSpec((None, kdim, tn), lambda j, i: (l, 0, j)),
            pl.BlockSpec((tm, tn), lambda j, i: (i, j)),
        ],
        out_specs=pl.BlockSpec((tm, tn), lambda j, i: (i, j)),
        out_shape=jax.ShapeDtypeStruct((m, n), F32),
        compiler_params=_cparams(("parallel", "arbitrary")),
        name="attn_out_proj",
    )(a, w, res)


def _rope_proj_kernel(a_ref, w_ref, cos_ref, sa_ref, sb_ref, *outs, rope, scale, transposed):
    z = jnp.dot(a_ref[...], w_ref[...], preferred_element_type=F32)
    if rope:
        cos, sa, sb = cos_ref[...], sa_ref[...], sb_ref[...]
        cols = []
        for c in range(z.shape[1] // LANES):
            zc = z[:, c * LANES:(c + 1) * LANES]
            up = pltpu.roll(zc, LANES - ROT_DIM // 2, 1)
            dn = pltpu.roll(zc, ROT_DIM // 2, 1)
            cols.append(zc * cos + up * sa + dn * sb)
        z = jnp.concatenate(cols, axis=1)
    if scale != 1.0:
        z = z * scale
    zt = z.T if any(transposed) else None
    for o, t in zip(outs, transposed):
        o[...] = (zt if t else z).astype(o.dtype)


def _rope_proj(a, w, w_index, n, tables, *, tm, tn, rope, scale=1.0, outs=((BF16, False),), batch=1):
    m, kdim = a.shape
    n_tab = tables[0].shape[0] // tm
    tab_spec = pl.BlockSpec((tm, LANES), lambda j, i: (i % n_tab, 0))
    bpb = m // batch // tm
    out_specs, out_shape = [], []
    for dt, t in outs:
        if t:
            out_shape.append(jax.ShapeDtypeStruct((batch, n, m // batch), dt))
            out_specs.append(pl.BlockSpec((None, tn, tm), lambda j, i: (i // bpb, j, i % bpb)))
        else:
            out_shape.append(jax.ShapeDtypeStruct((m, n), dt))
            out_specs.append(pl.BlockSpec((tm, tn), lambda j, i: (i, j)))
    w_block = (None,) * (w.ndim - 2) + (kdim, tn)
    return pl.pallas_call(
        functools.partial(_rope_proj_kernel, rope=rope, scale=scale, transposed=tuple(t for _, t in outs)),
        grid=(n // tn, m // tm),
        in_specs=[
            pl.BlockSpec((tm, kdim), lambda j, i: (i, 0)),
            pl.BlockSpec(w_block, lambda j, i: w_index(j)),
            tab_spec, tab_spec, tab_spec,
        ],
        out_specs=out_specs,
        out_shape=out_shape,
        compiler_params=_cparams(("parallel", "arbitrary")),
        name="rope_proj" if rope else "plain_proj",
    )(a, w, *tables)


def _rope_tables(pos):
    inv = ROPE_THETA ** (-jnp.arange(0, ROT_DIM, 2, dtype=F32) / ROT_DIM)
    ang = pos.astype(F32)[:, None] * inv[None, :]
    cos, sin = jnp.cos(ang), jnp.sin(ang)
    half = ROT_DIM // 2
    n = pos.shape[0]
    ones = jnp.ones((n, QK_DIM - ROT_DIM), F32)
    zeros = jnp.zeros((n, QK_DIM - half), F32)
    cos64 = jnp.concatenate([cos, cos, ones], axis=1)
    sa64 = jnp.concatenate([-sin, zeros], axis=1)
    sb64 = jnp.concatenate([jnp.zeros((n, half), F32), sin, jnp.zeros((n, QK_DIM - ROT_DIM), F32)], axis=1)
    rep = LANES // QK_DIM
    return tuple(jnp.tile(t, (1, rep)) for t in (cos64, sa64, sb64))


def _s5_step(ar, ai, hr, hi, xr, xi):
    return (ar * hr + xr) - ai * hi, (ar * hi + xi) + ai * hr


def _s5_scan_kernel(u_ref, h0_ref, bw_ref, cw_ref, a_ref, d_ref, g_ref, ht_ref,
                    x_scr, f_scr, hs_scr, *, seq, rc):
    n_t = seq // SUBLANES
    chunks = [(s, t0) for s in range(SUBLANES) for t0 in range(0, n_t, rc)]
    p = OCT_STATE
    nc = p // LANES
    bw = bw_ref[...]

    def lanes(c):
        return slice(c * LANES, (c + 1) * LANES)

    for s, t0 in chunks:
        x = jnp.dot(u_ref[pl.ds(s * n_t + t0, rc), :].astype(BF16), bw, preferred_element_type=F32)
        for c in range(2 * nc):
            x_scr[c, pl.ds(t0 * SUBLANES + s, rc, stride=SUBLANES), :] = x[:, lanes(c)]

    a = a_ref[...]
    ar = [jnp.broadcast_to(a[0:1, lanes(c)], (SUBLANES, LANES)) for c in range(nc)]
    ai = [jnp.broadcast_to(a[1:2, lanes(c)], (SUBLANES, LANES)) for c in range(nc)]

    def make_step(store):
        def step(t, carry):
            r0 = pl.multiple_of(t * SUBLANES, SUBLANES)
            new = []
            for c in range(nc):
                nr, ni = _s5_step(ar[c], ai[c], carry[c][0], carry[c][1],
                                  x_scr[c, pl.ds(r0, SUBLANES), :], x_scr[nc + c, pl.ds(r0, SUBLANES), :])
                if store:
                    x_scr[c, pl.ds(r0, SUBLANES), :] = nr
                    x_scr[nc + c, pl.ds(r0, SUBLANES), :] = ni
                new.append((nr, ni))
            return tuple(new)
        return step

    zero = jnp.zeros((SUBLANES, LANES), F32)
    fin = lax.fori_loop(0, n_t, make_step(False), tuple((zero, zero) for _ in range(nc)), unroll=8)
    for c in range(nc):
        f_scr[:, lanes(c)] = fin[c][0]
        f_scr[:, lanes(nc + c)] = fin[c][1]

    atr, ati = a[2:3], a[3:4]
    pr, pi = h0_ref[:, 0:p], h0_ref[:, p:2 * p]
    hs_scr[0:1, 0:p] = pr
    hs_scr[0:1, p:2 * p] = pi
    for s in range(1, SUBLANES):
        pr, pi = _s5_step(atr, ati, pr, pi, f_scr[s - 1:s, 0:p], f_scr[s - 1:s, p:2 * p])
        hs_scr[s:s + 1, 0:p] = pr
        hs_scr[s:s + 1, p:2 * p] = pi

    start = tuple((hs_scr[:, lanes(c)], hs_scr[:, lanes(nc + c)]) for c in range(nc))
    fin = lax.fori_loop(0, n_t, make_step(True), start, unroll=8)
    for c in range(nc):
        ht_ref[:, lanes(c)] = fin[c][0][SUBLANES - 1:SUBLANES]
        ht_ref[:, lanes(nc + c)] = fin[c][1][SUBLANES - 1:SUBLANES]

    cw = cw_ref[...]
    d = d_ref[...]

    for s, t0 in chunks:
        rows = pl.ds(s * n_t + t0, rc)
        h = jnp.concatenate([x_scr[c, pl.ds(t0 * SUBLANES + s, rc, stride=SUBLANES), :].astype(BF16)
                             for c in range(2 * nc)], axis=1)
        y = jnp.dot(h, cw, preferred_element_type=F32) + d * u_ref[rows, :]
        g_ref[rows, :] = jax.nn.gelu(y).astype(BF16)


def _s5_scan(u, h0, prm, *, rc):
    b, seq, d = u.shape
    n_oct = d // LANES
    p2 = 2 * OCT_STATE
    bw, cw, arows, drow = prm
    return pl.pallas_call(
        functools.partial(_s5_scan_kernel, seq=seq, rc=rc),
        grid=(b, n_oct),
        in_specs=[
            pl.BlockSpec((None, seq, LANES), lambda i, o: (i, 0, o)),
            pl.BlockSpec((None, None, 1, p2), lambda i, o: (i, o, 0, 0)),
            pl.BlockSpec((None, LANES, p2), lambda i, o: (o, 0, 0)),
            pl.BlockSpec((None, p2, LANES), lambda i, o: (o, 0, 0)),
            pl.BlockSpec((None, 4, OCT_STATE), lambda i, o: (o, 0, 0)),
            pl.BlockSpec((None, 1, LANES), lambda i, o: (o, 0, 0)),
        ],
        out_specs=[
            pl.BlockSpec((None, seq, LANES), lambda i, o: (i, 0, o)),
            pl.BlockSpec((None, None, 1, p2), lambda i, o: (i, o, 0, 0)),
        ],
        out_shape=[
            jax.ShapeDtypeStruct((b, seq, d), BF16),
            jax.ShapeDtypeStruct((b, n_oct, 1, p2), F32),
        ],
        scratch_shapes=[
            pltpu.VMEM((p2 // LANES, seq, LANES), F32),
            pltpu.VMEM((SUBLANES, p2), F32),
            pltpu.VMEM((SUBLANES, p2), F32),
        ],
        compiler_params=_cparams(("parallel", "arbitrary")),
        name="s5_scan",
    )(u, h0, bw, cw, arows, drow)


def _s5_single_kernel(u_ref, h0_ref, bw_ref, cw_ref, a_ref, d_ref, g_ref, hn_ref):
    p = OCT_STATE
    u = u_ref[...]
    x = jnp.dot(u.astype(BF16), bw_ref[...], preferred_element_type=F32)
    a = a_ref[...]
    hr, hi = _s5_step(a[0:1], a[1:2], h0_ref[:, 0:p], h0_ref[:, p:2 * p], x[:, 0:p], x[:, p:2 * p])
    hn_ref[:, 0:p] = hr
    hn_ref[:, p:2 * p] = hi
    y = jnp.dot(hn_ref[...].astype(BF16), cw_ref[...], preferred_element_type=F32) + d_ref[...] * u
    g_ref[...] = jax.nn.gelu(y).astype(BF16)


def _s5_single(u, h0, prm):
    m, d = u.shape
    n_oct = d // LANES
    p2 = 2 * OCT_STATE
    bw, cw, arows, drow = prm
    return pl.pallas_call(
        _s5_single_kernel,
        grid=(n_oct,),
        in_specs=[
            pl.BlockSpec((m, LANES), lambda o: (0, o)),
            pl.BlockSpec((None, m, p2), lambda o: (o, 0, 0)),
            pl.BlockSpec((None, LANES, p2), lambda o: (o, 0, 0)),
            pl.BlockSpec((None, p2, LANES), lambda o: (o, 0, 0)),
            pl.BlockSpec((None, 4, OCT_STATE), lambda o: (o, 0, 0)),
            pl.BlockSpec((None, 1, LANES), lambda o: (o, 0, 0)),
        ],
        out_specs=[
            pl.BlockSpec((m, LANES), lambda o: (0, o)),
            pl.BlockSpec((None, m, p2), lambda o: (o, 0, 0)),
        ],
        out_shape=[
            jax.ShapeDtypeStruct((m, d), BF16),
            jax.ShapeDtypeStruct((n_oct, m, p2), F32),
        ],
        compiler_params=_cparams(("parallel",)),
        name="s5_single",
    )(u, h0, bw, cw, arows, drow)


def _s5_params(a_re, a_im, log_dt, b_re, b_im, c_re, c_im, d, n_pow):
    g = a_re.shape[0]
    n_oct = g // OCT_GROUPS
    ar, ai = a_re.astype(F32), a_im.astype(F32)
    dt = jnp.exp(log_dt.astype(F32))[:, None]
    mag = jnp.exp(dt * ar)
    ab_re = mag * jnp.cos(dt * ai)
    ab_im = mag * jnp.sin(dt * ai)
    den = ar * ar + ai * ai
    n_re = ab_re - 1.0
    n_im = ab_im
    f_re = (n_re * ar + n_im * ai) / den
    f_im = (n_im * ar - n_re * ai) / den
    br, bi = b_re.astype(F32), b_im.astype(F32)
    bb_re = f_re[..., None] * br - f_im[..., None] * bi
    bb_im = f_re[..., None] * bi + f_im[..., None] * br
    eye = jnp.eye(OCT_GROUPS, dtype=F32)

    def in_blocks(bb):
        t = bb.reshape(n_oct, OCT_GROUPS, STATE_DIM, GROUP_CH)
        t = jnp.einsum('ogpc,gh->ogchp', t, eye)
        return t.reshape(n_oct, LANES, OCT_STATE)

    def out_blocks(cc):
        t = cc.astype(F32).reshape(n_oct, OCT_GROUPS, GROUP_CH, STATE_DIM)
        t = jnp.einsum('ogcp,gh->ogphc', t, eye)
        return t.reshape(n_oct, OCT_STATE, LANES)

    bw = jnp.concatenate([in_blocks(bb_re), in_blocks(bb_im)], axis=2).astype(BF16)
    cw = jnp.concatenate([out_blocks(c_re), -out_blocks(c_im)], axis=1).astype(BF16)
    pw_re, pw_im = ab_re, ab_im
    for _ in range(n_pow):
        pw_re, pw_im = pw_re * pw_re - pw_im * pw_im, 2.0 * pw_re * pw_im
    arows = jnp.stack([t.reshape(n_oct, OCT_STATE) for t in (ab_re, ab_im, pw_re, pw_im)], axis=1)
    drow = d.astype(F32).reshape(n_oct, 1, LANES)
    return bw, cw, arows, drow


def _lambda(lp, lam_init):
    a = jnp.sum(lp[0:1] * lp[1:2], axis=1, keepdims=True)
    b = jnp.sum(lp[2:3] * lp[3:4], axis=1, keepdims=True)
    return jnp.exp(a) - jnp.exp(b) + lam_init


def _attn_kernel(qtab, ktab, pt_ref, qt_ref, k_ref, vt_ref, lp_ref, sgc_ref, sgr_ref, qcol_ref, *refs,
                 pages, groups, n_seq, n_heads, lam_init):
    k_pages, v_pages = refs[:pages], refs[pages:2 * pages]
    (kn_ref, vn_ref, rep_ref, o_ref, os_ref,
     m_scr, l_scr, a_scr, s_scr, ms_scr, ls_scr, as_scr) = refs[2 * pages:]
    step = pl.program_id(2)
    qi, ki = qtab[step], ktab[step]
    heads = k_ref.shape[1] // LANES
    n_tok = kn_ref.shape[1]
    lin = (pl.program_id(0) * pl.num_programs(1) + pl.program_id(1)) * pl.num_programs(2) + step
    active = lin < n_seq * groups
    grp = lax.rem(jnp.minimum(lin, n_seq * groups - 1), groups)

    @pl.when(ki == 0)
    def _():
        m_scr[...] = jnp.full(m_scr.shape, NEG_INF, F32)
        l_scr[...] = jnp.zeros(l_scr.shape, F32)
        a_scr[...] = jnp.zeros(a_scr.shape, F32)

    @pl.when(active & (grp == 0))
    def _():
        ms_scr[...] = jnp.full(ms_scr.shape, NEG_INF, F32)
        ls_scr[...] = jnp.zeros(ls_scr.shape, F32)
        as_scr[...] = jnp.zeros(as_scr.shape, F32)

    def sample_page(kt_ref, v_ref, n_valid):
        rows = 2 * n_heads
        rowi = lax.broadcasted_iota(jnp.int32, (rows, n_tok * n_heads), 0)
        coli = lax.broadcasted_iota(jnp.int32, (rows, n_tok * n_heads), 1)
        head_sel = (coli & (n_heads - 1)) == (rowi >> 1)
        prod = kt_ref[...] * qcol_ref[...]
        for c in range(rows):
            s_scr[c:c + 1, :] = jnp.sum(prod[c * QK_DIM:(c + 1) * QK_DIM, :], axis=0, keepdims=True)
        s = s_scr[...]
        s = jnp.where(lax.broadcasted_iota(jnp.int32, s.shape, 1) < n_valid, s, NEG_INF)
        m_old = ms_scr[...]
        m_new = jnp.maximum(m_old, jnp.max(s, axis=1, keepdims=True))
        alpha = jnp.exp2(m_old - m_new)
        p = jnp.exp2(s - m_new)
        ls_scr[...] = alpha * ls_scr[...] + jnp.sum(p, axis=1, keepdims=True)
        spread = jnp.dot(p.astype(BF16), rep_ref[...], preferred_element_type=F32)
        spread = jnp.where(head_sel, spread, 0.0).astype(BF16)
        pv = jnp.dot(spread, v_ref[...].astype(BF16), preferred_element_type=F32)
        as_scr[...] = alpha * as_scr[...] + pv
        ms_scr[...] = m_new

    def sample_pages():
        n_valid = jnp.where(active, n_tok, 0)
        for i in range(pages):
            sample_page(k_pages[i], v_pages[i], n_valid)

    def block(diagonal):
        tk, tq = k_ref.shape[0], qt_ref.shape[1]
        feat = lax.broadcasted_iota(jnp.int32, (LANES, tq), 0)
        if diagonal:
            keep = (lax.broadcasted_iota(jnp.int32, (tk, tq), 0)
                    <= lax.broadcasted_iota(jnp.int32, (tk, tq), 1))
        scores = []
        for h in range(heads):
            qt = qt_ref[h * LANES:(h + 1) * LANES, :]
            k = k_ref[:, h * LANES:(h + 1) * LANES]
            zero = jnp.zeros_like(qt)
            for qm in (jnp.where(feat < QK_DIM, qt, zero), jnp.where(feat >= QK_DIM, qt, zero)):
                scores.append(jnp.dot(k, qm, preferred_element_type=F32))
        probs, alphas = [], []
        for c, st in enumerate(scores):
            if diagonal:
                st = jnp.where(keep, st, NEG_INF)
            m_old = m_scr[c:c + 1, :]
            m_new = jnp.maximum(m_old, jnp.max(st, axis=0, keepdims=True))
            alpha = jnp.exp2(m_old - m_new)
            pt = jnp.exp2(st - m_new)
            l_scr[c:c + 1, :] = alpha * l_scr[c:c + 1, :] + jnp.sum(pt, axis=0, keepdims=True)
            m_scr[c:c + 1, :] = m_new
            probs.append(pt.astype(BF16))
            alphas.append(alpha)
        for c, (pt, alpha) in enumerate(zip(probs, alphas)):
            h = c // 2
            rows = slice(c * V_DIM, (c + 1) * V_DIM)
            vt = vt_ref[h * LANES:(h + 1) * LANES, :]
            a_scr[rows, :] = alpha * a_scr[rows, :] + jnp.dot(vt, pt, preferred_element_type=F32)

    @pl.when(ki < qi)
    def _():
        block(False)
        sample_pages()

    @pl.when(ki == qi)
    def _():
        block(True)
        sample_pages()
        lam = _lambda(lp_ref[...], lam_init)
        for h in range(heads):
            c = 2 * h
            o1 = a_scr[c * V_DIM:(c + 1) * V_DIM, :] / l_scr[c:c + 1, :]
            o2 = a_scr[(c + 1) * V_DIM:(c + 2) * V_DIM, :] / l_scr[c + 1:c + 2, :]
            ot = o1 - lam * o2
            ms = jnp.mean(ot * ot, axis=0, keepdims=True)
            ot = ot * lax.rsqrt(ms + NORM_EPS) * sgc_ref[...] * (1.0 - lam_init)
            o_ref[:, h * LANES:(h + 1) * LANES] = ot.T.astype(o_ref.dtype)

    @pl.when(active & (grp == groups - 1))
    def _():
        sample_page(kn_ref, vn_ref, 1)
        lam = _lambda(lp_ref[...], lam_init)
        s_scr[...] = as_scr[...] / ls_scr[...]
        o = s_scr[pl.ds(0, n_heads, stride=2), :] - lam * s_scr[pl.ds(1, n_heads, stride=2), :]
        os_ref[...] = (_rms(o, sgr_ref[...]) * (1.0 - lam_init)).astype(os_ref.dtype)


def _attention(qt, k, vt, qcol, kn, vn, cache_kt, cache_v2, page_table, lp, sg, *, tq, lam_init):
    b, seq, width = k.shape
    n_heads = width // LANES
    hp = FLASH_HEADS if n_heads % FLASH_HEADS == 0 else 1
    hw = hp * LANES
    nq = seq // tq
    pairs = [(i, j) for i in range(nq) for j in range(i + 1)]
    qtab = jnp.asarray([p[0] for p in pairs], jnp.int32)
    ktab = jnp.asarray([p[1] for p in pairs], jnp.int32)
    n_hg, n_pairs = n_heads // hp, len(pairs)

    n_seq, n_pages = page_table.shape
    page = cache_kt.shape[2]
    pages = PAGES_PER_STEP if n_pages % PAGES_PER_STEP == 0 else 1
    groups = n_pages // pages
    assert n_heads & (n_heads - 1) == 0 and page == LANES and n_seq * groups <= b * n_hg * n_pairs
    rep = (jnp.arange(page * n_heads)[None, :] // n_heads == jnp.arange(page)[:, None]).astype(BF16)

    def seq_grp(i, h, s):
        lin = jnp.minimum((i * n_hg + h) * n_pairs + s, n_seq * groups - 1)
        return lax.div(lin, groups), lax.rem(lin, groups)

    def per_seq(shape):
        return pl.BlockSpec((None,) + shape, lambda i, h, s, qt_, kt_, pt: (seq_grp(i, h, s)[0], 0, 0))

    def page_spec(shape, n):
        def index(i, h, s, qt_, kt_, pt):
            sq, g = seq_grp(i, h, s)
            return pt[sq, g * pages + n], 0, 0
        return pl.BlockSpec((None,) + shape, index)

    const = lambda shape: pl.BlockSpec(shape, lambda i, h, s, qt_, kt_, pt: (0, 0))
    grid_spec = pltpu.PrefetchScalarGridSpec(
        num_scalar_prefetch=3,
        grid=(b, n_hg, n_pairs),
        in_specs=[
            pl.BlockSpec((None, hw, tq), lambda i, h, s, qt_, kt_, pt: (i, h, qt_[s])),
            pl.BlockSpec((None, tq, hw), lambda i, h, s, qt_, kt_, pt: (i, kt_[s], h)),
            pl.BlockSpec((None, hw, tq), lambda i, h, s, qt_, kt_, pt: (i, h, kt_[s])),
            const((4, QK_DIM)), const((V_DIM, 1)), const((1, V_DIM)),
            per_seq((width, page))]
        + [page_spec((width, page), n) for n in range(pages)]
        + [page_spec((page * n_heads, V_DIM), n) for n in range(pages)]
        + [per_seq((width, page)), per_seq((page * n_heads, V_DIM)), const((page, page * n_heads))],
        out_specs=[
            pl.BlockSpec((None, tq, hw), lambda i, h, s, qt_, kt_, pt: (i, qt_[s], h)),
            per_seq((n_heads, V_DIM)),
        ],
        scratch_shapes=[
            pltpu.VMEM((2 * hp, tq), F32), pltpu.VMEM((2 * hp, tq), F32),
            pltpu.VMEM((2 * hp * V_DIM, tq), F32),
            pltpu.VMEM((2 * n_heads, page), F32),
            pltpu.VMEM((2 * n_heads, 1), F32), pltpu.VMEM((2 * n_heads, 1), F32),
            pltpu.VMEM((2 * n_heads, V_DIM), F32),
        ],
    )
    return pl.pallas_call(
        functools.partial(_attn_kernel, pages=pages, groups=groups, n_seq=n_seq, n_heads=n_heads,
                          lam_init=lam_init),
        grid_spec=grid_spec,
        out_shape=[jax.ShapeDtypeStruct((b, seq, width), BF16),
                   jax.ShapeDtypeStruct((n_seq, n_heads, V_DIM), BF16)],
        compiler_params=_cparams(("arbitrary", "arbitrary", "arbitrary")),
        name="diff_attention",
    )(qtab, ktab, page_table, qt, k, vt, lp, sg.reshape(V_DIM, 1), sg.reshape(1, V_DIM), qcol,
      *([cache_kt] * pages), *([cache_v2] * pages), kn, vn, rep)


def _prep_weights(ffn_w_in, ffn_w_out, s5, s5_w_glu, w_kv, attn_w_q, attn_w_o, n_pow):
    assert ffn_w_out.shape[2] >= FF_TILE
    s5_prm = [_s5_params(*(t[l] for t in s5), n_pow) for l in range(s5_w_glu.shape[0])]
    return ((ffn_w_in.astype(BF16), ffn_w_out.astype(BF16)), s5_prm, s5_w_glu.astype(BF16),
            w_kv.astype(BF16), attn_w_q.astype(BF16), attn_w_o.astype(BF16))


def _trunk(x, prompt, weights, gains, h0, past_len):
    ffn_w, s5_prm, glu_w, wkv, wq, wo = weights
    norm_g, kv_norm_g, final_norm_g = gains
    depth = norm_g.shape[0]
    n_a = len(s5_prm)
    if prompt:
        b, seq, d = x.shape
        m = b * seq
        tm = min(ROW_TILE, seq // SUBLANES)
        pos = jnp.arange(seq)
    else:
        m, d = x.shape
        b, tm = 1, m
        pos = jnp.full((m,), past_len, jnp.int32)
    tn = min(COL_TILE, d)
    qk_blocks = wq.shape[2] // tn
    tables = _rope_tables(pos)
    xf = x.reshape(m, d)
    plain = lambda dt: (jax.ShapeDtypeStruct((m, d), dt), pl.BlockSpec((tm, d), lambda i, j: (i, 0)))
    proj = functools.partial(_rope_proj, tables=tables, tm=tm, tn=tn, batch=b)
    states, k_out, v_out, k_att, v_att, hn = [], None, None, None, None, None
    for l in range(depth):
        if l == n_a:
            k_outs = ((BF16, False), (F32, True)) if prompt else ((F32, False),)
            v_outs = ((F32, False), (BF16, True)) if prompt else ((F32, False),)
            k_res = proj(hn, wkv, lambda j: (0, j), wq.shape[2], rope=True, outs=k_outs)
            v_res = proj(hn, wkv, lambda j: (0, qk_blocks + j), wkv.shape[1] - wq.shape[2], rope=False, outs=v_outs)
            k_att, k_out = k_res if prompt else (None, k_res[0])
            v_out, v_att = v_res if prompt else (v_res[0], None)
        xf, hn = _ffn(xf, norm_g[l, 0], ffn_w, (l, 0), norm_g[l, 1], tm=tm, norm=plain(F32 if l < n_a else BF16))
        if l < n_a:
            if prompt:
                g, st = _s5_scan(hn.reshape(b, seq, d), h0[l], s5_prm[l], rc=tm)
                g = g.reshape(m, d)
            else:
                g, st = _s5_single(hn, h0[l], s5_prm[l])
            states.append(st)
            xf = _glu_proj(g, glu_w, l, xf, tm=tm, tn=min(GLU_COL_TILE, d))
        else:
            jl = l - n_a
            q_outs = ((BF16, True),) if prompt else ((BF16, False),)
            q = proj(hn, wq, lambda j, jl=jl: (jl, 0, j), wq.shape[2], rope=True,
                     scale=Q_SCALE * LOG2_E, outs=q_outs)[0]
            o = yield q, k_att, v_att, k_out, v_out
            xf = _res_proj(o, wo, jl, xf, tm=tm, tn=tn)
        if l == n_a - 1:
            xf, hn = _ffn(xf, norm_g[l, 2], ffn_w, (l, 1), kv_norm_g, tm=tm, norm=plain(BF16))
        elif l == depth - 1:
            (y,) = _ffn(xf, norm_g[l, 2], ffn_w, (l, 1), final_norm_g, tm=tm, emit_x=False, norm=plain(F32))
        else:
            (xf,) = _ffn(xf, norm_g[l, 2], ffn_w, (l, 1), final_norm_g, tm=tm)
    return y, states, k_out, v_out


def _advance(gen, value):
    try:
        return (next(gen) if value is None else gen.send(value)), None
    except StopIteration as stop:
        return None, stop.value


def kernel(x_prompt, x_sample, state_ssm_re, state_ssm_im, cache_k, cache_v, page_table, norm_g, ffn_w_in, ffn_w_out, s5_a_re, s5_a_im, s5_log_dt, s5_b_re, s5_b_im, s5_c_re, s5_c_im, s5_d, s5_w_glu, kv_norm_g, w_kv, attn_w_q, attn_w_o, diff_lambda, subln_g, final_norm_g):
    b, seq, d = x_prompt.shape
    n_seq = x_sample.shape[0]
    n_a, _, n_groups, p = state_ssm_re.shape
    n_oct = n_groups // OCT_GROUPS
    n_phys, page, n_heads = cache_v.shape[:3]
    n_t = seq // SUBLANES
    n_pow = n_t.bit_length() - 1
    assert seq == SUBLANES * (1 << n_pow) and x_sample.shape[1] == 1 and n_seq <= SAMPLE_ROWS

    s5 = (s5_a_re, s5_a_im, s5_log_dt, s5_b_re, s5_b_im, s5_c_re, s5_c_im, s5_d)
    weights = _prep_weights(ffn_w_in, ffn_w_out, s5, s5_w_glu, w_kv, attn_w_q, attn_w_o, n_pow)
    gains = (norm_g, kv_norm_g, final_norm_g)

    h0_p = jnp.zeros((n_a, b, n_oct, 1, 2 * OCT_STATE), F32)
    prompt = _trunk(x_prompt, True, weights, gains, h0_p, None)

    pad = SAMPLE_ROWS - n_seq
    xs = jnp.pad(x_sample.reshape(n_seq, d), ((0, pad), (0, 0)))

    def block_sample(t):
        t = t.reshape(n_a, n_seq, n_oct, OCT_STATE).transpose(0, 2, 1, 3)
        return jnp.pad(t, ((0, 0), (0, 0), (0, pad), (0, 0)))

    h0_s = jnp.concatenate([block_sample(state_ssm_re), block_sample(state_ssm_im)], axis=-1)
    cache_kt = jnp.transpose(cache_k, (0, 2, 3, 4, 1)).reshape(n_phys, -1, page)
    cache_v2 = cache_v.reshape(n_phys, page * n_heads, V_DIM)
    width = cache_kt.shape[1]
    sample = _trunk(xs, False, weights, gains, h0_s, page_table.shape[1] * page)

    (req_p, res_p), (req_s, res_s) = _advance(prompt, None), _advance(sample, None)
    jl = 0
    while req_p is not None:
        lam_init = 0.8 - 0.6 * math.exp(-0.3 * (n_a + jl))
        qt, k_att, vt_att = req_p[:3]
        q_new, _, _, k_new, v_new = req_s
        qcol = jnp.broadcast_to(q_new[:n_seq].astype(F32)[:, :, None], (n_seq, width, page))
        kn = jnp.zeros((n_seq, width, page), F32).at[:, :, 0].set(k_new[:n_seq])
        vn = jnp.zeros((n_seq, page * n_heads, V_DIM), F32).at[:, :n_heads, :].set(
            v_new[:n_seq].reshape(n_seq, n_heads, V_DIM))
        o_p, o_s = _attention(qt, k_att.reshape(b, seq, width), vt_att, qcol, kn, vn, cache_kt, cache_v2,
                              page_table, diff_lambda[jl], subln_g[jl],
                              tq=min(ROW_TILE, seq // SUBLANES), lam_init=lam_init)
        o_s = jnp.pad(o_s.reshape(n_seq, width), ((0, pad), (0, 0)))
        (req_p, res_p), (req_s, res_s) = _advance(prompt, o_p.reshape(b * seq, width)), _advance(sample, o_s)
        jl += 1
    y_p, st_p, kt_p, v_p = res_p
    y_s, st_s, k_s, v_s = res_s
    k_p = jnp.transpose(kt_p.reshape(b, n_heads, 2, QK_DIM, seq), (0, 4, 1, 2, 3))

    def unblock_prompt(st, half):
        t = st[:, :, 0, half * OCT_STATE:(half + 1) * OCT_STATE]
        return t.reshape(b, n_groups, p)

    def unblock_sample(st, half):
        t = st[:, :n_seq, half * OCT_STATE:(half + 1) * OCT_STATE]
        return t.transpose(1, 0, 2).reshape(n_seq, n_groups, p)

    re_p = jnp.stack([unblock_prompt(s, 0) for s in st_p])
    im_p = jnp.stack([unblock_prompt(s, 1) for s in st_p])
    re_s = jnp.stack([unblock_sample(s, 0) for s in st_s])
    im_s = jnp.stack([unblock_sample(s, 1) for s in st_s])

    return (y_p.reshape(b, seq, d), y_s[:n_seq].reshape(n_seq, 1, d), re_p, im_p,
            k_p, v_p.reshape(b, seq, n_heads, V_DIM),
            re_s, im_s,
            k_s[:n_seq].reshape(n_seq, 1, n_heads, 2, QK_DIM), v_s[:n_seq].reshape(n_seq, 1, n_heads, V_DIM))
```

```python
import functools
import math

import jax
import jax.numpy as jnp
from jax import lax
from jax.experimental import pallas as pl
from jax.experimental.pallas import tpu as pltpu

F32 = jnp.float32
BF16 = jnp.bfloat16

GROUP_CH = 16
STATE_DIM = 64
QK_DIM = 64
V_DIM = 2 * QK_DIM
ROT_DIM = QK_DIM // 4
ROPE_THETA = 500000.0
NORM_EPS = 1e-6
NEG_INF = -1e30
Q_SCALE = QK_DIM ** -0.5
LOG2_E = 1.4426950408889634

LANES = 128
SUBLANES = 8
OCT_GROUPS = LANES // GROUP_CH
OCT_STATE = OCT_GROUPS * STATE_DIM
VMEM_LIMIT = 56 * 1024 * 1024

ROW_TILE = 512
COL_TILE = 2048
GLU_COL_TILE = 1024
FF_TILE = 512
FLASH_HEADS = 4
SAMPLE_ROWS = 16
PAGES_PER_STEP = 4


def _cparams(sem):
    return pltpu.CompilerParams(dimension_semantics=sem, vmem_limit_bytes=VMEM_LIMIT)


def _rms(x, g):
    return x * lax.rsqrt(jnp.mean(x * x, axis=-1, keepdims=True) + NORM_EPS) * g


def _sigmoid(x):
    return 1.0 / (1.0 + jnp.exp(-x))


def _ffn_kernel(x_ref, g_ref, gn_ref, wa_ref, wb_ref, wo_ref, *rest, n_main, has_tail, emit_x, emit_norm):
    tails, rest = (rest[:3], rest[3:]) if has_tail else ((), rest)
    outs, (h_scr, acc_scr) = rest[:-2], rest[-2:]
    j = pl.program_id(1)

    @pl.when(j == 0)
    def _():
        h_scr[...] = _rms(x_ref[...], g_ref[...]).astype(BF16)
        acc_scr[...] = jnp.zeros(acc_scr.shape, F32)

    def hidden(wa, wb, wo):
        h = h_scr[...]
        za = jnp.dot(h, wa[0, 0], preferred_element_type=F32)
        zb = jnp.dot(h, wb[0, 0], preferred_element_type=F32)
        act = (za * _sigmoid(za)) * zb
        return jnp.dot(act.astype(BF16), wo[0, 0], preferred_element_type=F32)

    acc_scr[...] += hidden(wa_ref, wb_ref, wo_ref)

    @pl.when(j == n_main - 1)
    def _():
        acc = acc_scr[...]
        if has_tail:
            acc = acc + hidden(*tails)
        xo = x_ref[...] + 0.5 * acc
        k = 0
        if emit_x:
            outs[k][...] = xo
            k += 1
        if emit_norm:
            outs[k][...] = _rms(xo, gn_ref[...]).astype(outs[k].dtype)


def _ffn(x, g, w, lk, gn, *, tm, emit_x=True, norm=None):
    m, d = x.shape
    l, k = lk
    w_in, w_out = w
    d_ff = w_out.shape[2]
    n_main = d_ff // FF_TILE
    main = n_main * FF_TILE
    tail = d_ff - main
    out_shape, out_specs = [], []
    if emit_x:
        out_shape.append(jax.ShapeDtypeStruct((m, d), F32))
        out_specs.append(pl.BlockSpec((tm, d), lambda i, j: (i, 0)))
    if norm is not None:
        out_shape.append(norm[0])
        out_specs.append(norm[1])
    row = pl.BlockSpec((1, d), lambda i, j: (0, 0))
    one, full = pl.Element(1), pl.Element(d)

    assert d_ff % LANES == 0 and FF_TILE % LANES == 0

    def offset(j, start, step):
        return (start // LANES + j * (step // LANES)) * LANES

    def cols(size, start, step):
        return pl.BlockSpec((one, one, full, pl.Element(size)),
                            lambda i, j: (l, k, 0, offset(j, start, step)))

    def rows(size, start, step):
        return pl.BlockSpec((one, one, pl.Element(size), full),
                            lambda i, j: (l, k, offset(j, start, step), 0))

    in_specs = [pl.BlockSpec((tm, d), lambda i, j: (i, 0)), row, row,
                cols(FF_TILE, 0, FF_TILE), cols(FF_TILE, d_ff, FF_TILE), rows(FF_TILE, 0, FF_TILE)]
    operands = [w_in, w_in, w_out]
    if tail:
        in_specs += [cols(tail, main, 0), cols(tail, d_ff + main, 0), rows(tail, main, 0)]
        operands += [w_in, w_in, w_out]
    return pl.pallas_call(
        functools.partial(_ffn_kernel, n_main=n_main, has_tail=bool(tail), emit_x=emit_x,
                          emit_norm=norm is not None),
        grid=(m // tm, n_main),
        in_specs=in_specs,
        out_specs=out_specs,
        out_shape=out_shape,
        scratch_shapes=[pltpu.VMEM((tm, d), BF16), pltpu.VMEM((tm, d), F32)],
        compiler_params=_cparams(("parallel", "arbitrary")),
        name="ffn",
    )(x, g.reshape(1, d), gn.reshape(1, d), *operands)


def _glu_kernel(a_ref, w1_ref, w2_ref, res_ref, o_ref):
    a = a_ref[...]
    z1 = jnp.dot(a, w1_ref[...], preferred_element_type=F32)
    z2 = jnp.dot(a, w2_ref[...], preferred_element_type=F32)
    o_ref[...] = res_ref[...] + z1 * _sigmoid(z2)


def _glu_proj(a, w, l, res, *, tm, tn):
    m, n = res.shape
    kdim = w.shape[1]
    nb = n // tn
    return pl.pallas_call(
        _glu_kernel,
        grid=(nb, m // tm),
        in_specs=[
            pl.BlockSpec((tm, kdim), lambda j, i: (i, 0)),
            pl.BlockSpec((None, kdim, tn), lambda j, i: (l, 0, j)),
            pl.BlockSpec((None, kdim, tn), lambda j, i: (l, 0, nb + j)),
            pl.BlockSpec((tm, tn), lambda j, i: (i, j)),
        ],
        out_specs=pl.BlockSpec((tm, tn), lambda j, i: (i, j)),
        out_shape=jax.ShapeDtypeStruct((m, n), F32),
        compiler_params=_cparams(("parallel", "arbitrary")),
        name="s5_glu",
    )(a, w, w, res)


def _res_proj_kernel(a_ref, w_ref, res_ref, o_ref):
    o_ref[...] = res_ref[...] + jnp.dot(a_ref[...], w_ref[...], preferred_element_type=F32)


def _res_proj(a, w, l, res, *, tm, tn):
    m, n = res.shape
    kdim = w.shape[1]
    return pl.pallas_call(
        _res_proj_kernel,
        grid=(n // tn, m // tm),
        in_specs=[
            pl.BlockSpec((tm, kdim), lambda j, i: (i, 0)),
            pl.BlockSpec((None, kdim, tn), lambda j, i: (l, 0, j)),
            pl.BlockSpec((tm, tn), lambda j, i: (i, j)),
        ],
        out_specs=pl.BlockSpec((tm, tn), lambda j, i: (i, j)),
        out_shape=jax.ShapeDtypeStruct((m, n), F32),
        compiler_params=_cparams(("parallel", "arbitrary")),
        name="attn_out_proj",
    )(a, w, res)


def _rope_proj_kernel(a_ref, w_ref, cos_ref, sa_ref, sb_ref, *outs, rope, scale, transposed):
    z = jnp.dot(a_ref[...], w_ref[...], preferred_element_type=F32)
    if rope:
        cos, sa, sb = cos_ref[...], sa_ref[...], sb_ref[...]
        cols = []
        for c in range(z.shape[1] // LANES):
            zc = z[:, c * LANES:(c + 1) * LANES]
            up = pltpu.roll(zc, LANES - ROT_DIM // 2, 1)
            dn = pltpu.roll(zc, ROT_DIM // 2, 1)
            cols.append(zc * cos + up * sa + dn * sb)
        z = jnp.concatenate(cols, axis=1)
    if scale != 1.0:
        z = z * scale
    zt = z.T if any(transposed) else None
    for o, t in zip(outs, transposed):
        o[...] = (zt if t else z).astype(o.dtype)


def _rope_proj(a, w, w_index, n, tables, *, tm, tn, rope, scale=1.0, outs=((BF16, False),), batch=1):
    m, kdim = a.shape
    n_tab = tables[0].shape[0] // tm
    tab_spec = pl.BlockSpec((tm, LANES), lambda j, i: (i % n_tab, 0))
    bpb = m // batch // tm
    out_specs, out_shape = [], []
    for dt, t in outs:
        if t:
            out_shape.append(jax.ShapeDtypeStruct((batch, n, m // batch), dt))
            out_specs.append(pl.BlockSpec((None, tn, tm), lambda j, i: (i // bpb, j, i % bpb)))
        else:
            out_shape.append(jax.ShapeDtypeStruct((m, n), dt))
            out_specs.append(pl.BlockSpec((tm, tn), lambda j, i: (i, j)))
    w_block = (None,) * (w.ndim - 2) + (kdim, tn)
    return pl.pallas_call(
        functools.partial(_rope_proj_kernel, rope=rope, scale=scale, transposed=tuple(t for _, t in outs)),
        grid=(n // tn, m // tm),
        in_specs=[
            pl.BlockSpec((tm, kdim), lambda j, i: (i, 0)),
            pl.BlockSpec(w_block, lambda j, i: w_index(j)),
            tab_spec, tab_spec, tab_spec,
        ],
        out_specs=out_specs,
        out_shape=out_shape,
        compiler_params=_cparams(("parallel", "arbitrary")),
        name="rope_proj" if rope else "plain_proj",
    )(a, w, *tables)


def _rope_tables(pos):
    inv = ROPE_THETA ** (-jnp.arange(0, ROT_DIM, 2, dtype=F32) / ROT_DIM)
    ang = pos.astype(F32)[:, None] * inv[None, :]
    cos, sin = jnp.cos(ang), jnp.sin(ang)
    half = ROT_DIM // 2
    n = pos.shape[0]
    ones = jnp.ones((n, QK_DIM - ROT_DIM), F32)
    zeros = jnp.zeros((n, QK_DIM - half), F32)
    cos64 = jnp.concatenate([cos, cos, ones], axis=1)
    sa64 = jnp.concatenate([-sin, zeros], axis=1)
    sb64 = jnp.concatenate([jnp.zeros((n, half), F32), sin, jnp.zeros((n, QK_DIM - ROT_DIM), F32)], axis=1)
    rep = LANES // QK_DIM
    return tuple(jnp.tile(t, (1, rep)) for t in (cos64, sa64, sb64))


def _s5_step(ar, ai, hr, hi, xr, xi):
    return (ar * hr + xr) - ai * hi, (ar * hi + xi) + ai * hr


def _s5_scan_kernel(u_ref, h0_ref, bw_ref, cw_ref, a_ref, d_ref, g_ref, ht_ref, x_scr, h_scr):
    n_t = u_ref.shape[0]
    nc = OCT_STATE // LANES

    def lanes(c):
        return slice(c * LANES, (c + 1) * LANES)

    def interleaved(r):
        return pl.ds(r, n_t, stride=SUBLANES)

    @pl.when(pl.program_id(2) == 0)
    def _():
        h_scr[...] = h0_ref[...]

    for r in range(SUBLANES):
        x = jnp.dot(u_ref[:, lanes(r)].astype(BF16), bw_ref[r], preferred_element_type=F32)
        for c in range(2 * nc):
            x_scr[c, interleaved(r), :] = x[:, lanes(c)]

    ar = [a_ref[0, :, lanes(c)] for c in range(nc)]
    ai = [a_ref[1, :, lanes(c)] for c in range(nc)]

    def step(t, carry):
        r0 = pl.multiple_of(t * SUBLANES, SUBLANES)
        new = []
        for c in range(nc):
            nr, ni = _s5_step(ar[c], ai[c], carry[c][0], carry[c][1],
                              x_scr[c, pl.ds(r0, SUBLANES), :], x_scr[nc + c, pl.ds(r0, SUBLANES), :])
            x_scr[c, pl.ds(r0, SUBLANES), :] = nr
            x_scr[nc + c, pl.ds(r0, SUBLANES), :] = ni
            new.append((nr, ni))
        return tuple(new)

    start = tuple((h_scr[:, lanes(c)], h_scr[:, lanes(nc + c)]) for c in range(nc))
    fin = lax.fori_loop(0, n_t, step, start, unroll=8)
    for c in range(nc):
        h_scr[:, lanes(c)] = fin[c][0]
        h_scr[:, lanes(nc + c)] = fin[c][1]
    ht_ref[...] = h_scr[...]

    for r in range(SUBLANES):
        h = jnp.concatenate([x_scr[c, interleaved(r), :].astype(BF16) for c in range(2 * nc)], axis=1)
        y = jnp.dot(h, cw_ref[r], preferred_element_type=F32) + d_ref[:, lanes(r)] * u_ref[:, lanes(r)]
        g_ref[:, lanes(r)] = jax.nn.gelu(y).astype(BF16)


def _s5_scan(u, h0, prm, *, tb):
    b, seq, d = u.shape
    p2 = 2 * OCT_STATE
    wide = SUBLANES * LANES
    bw, cw, arows, _, drow = prm
    assert d % wide == 0
    return pl.pallas_call(
        _s5_scan_kernel,
        grid=(b, d // wide, seq // tb),
        in_specs=[
            pl.BlockSpec((None, tb, wide), lambda i, o, t: (i, t, o)),
            pl.BlockSpec((None, None, SUBLANES, p2), lambda i, o, t: (i, o, 0, 0)),
            pl.BlockSpec((SUBLANES, LANES, p2), lambda i, o, t: (o, 0, 0)),
            pl.BlockSpec((SUBLANES, p2, LANES), lambda i, o, t: (o, 0, 0)),
            pl.BlockSpec((2, SUBLANES, OCT_STATE), lambda i, o, t: (0, o, 0)),
            pl.BlockSpec((1, wide), lambda i, o, t: (0, o)),
        ],
        out_specs=[
            pl.BlockSpec((None, tb, wide), lambda i, o, t: (i, t, o)),
            pl.BlockSpec((None, None, SUBLANES, p2), lambda i, o, t: (i, o, 0, 0)),
        ],
        out_shape=[
            jax.ShapeDtypeStruct((b, seq, d), BF16),
            jax.ShapeDtypeStruct((b, d // wide, SUBLANES, p2), F32),
        ],
        scratch_shapes=[
            pltpu.VMEM((p2 // LANES, SUBLANES * tb, LANES), F32),
            pltpu.VMEM((SUBLANES, p2), F32),
        ],
        compiler_params=_cparams(("parallel", "parallel", "arbitrary")),
        name="s5_scan",
    )(u, h0, bw, cw, arows, drow.reshape(1, d))


def _s5_single_kernel(u_ref, h0_ref, bw_ref, cw_ref, a_ref, d_ref, g_ref, hn_ref):
    p = OCT_STATE
    u = u_ref[...]
    x = jnp.dot(u.astype(BF16), bw_ref[...], preferred_element_type=F32)
    a = a_ref[...]
    hr, hi = _s5_step(a[0:1], a[1:2], h0_ref[:, 0:p], h0_ref[:, p:2 * p], x[:, 0:p], x[:, p:2 * p])
    hn_ref[:, 0:p] = hr
    hn_ref[:, p:2 * p] = hi
    y = jnp.dot(hn_ref[...].astype(BF16), cw_ref[...], preferred_element_type=F32) + d_ref[...] * u
    g_ref[...] = jax.nn.gelu(y).astype(BF16)


def _s5_single(u, h0, prm):
    m, d = u.shape
    n_oct = d // LANES
    p2 = 2 * OCT_STATE
    bw, cw, _, arows, drow = prm
    return pl.pallas_call(
        _s5_single_kernel,
        grid=(n_oct,),
        in_specs=[
            pl.BlockSpec((m, LANES), lambda o: (0, o)),
            pl.BlockSpec((None, m, p2), lambda o: (o, 0, 0)),
            pl.BlockSpec((None, LANES, p2), lambda o: (o, 0, 0)),
            pl.BlockSpec((None, p2, LANES), lambda o: (o, 0, 0)),
            pl.BlockSpec((None, 2, OCT_STATE), lambda o: (o, 0, 0)),
            pl.BlockSpec((None, 1, LANES), lambda o: (o, 0, 0)),
        ],
        out_specs=[
            pl.BlockSpec((m, LANES), lambda o: (0, o)),
            pl.BlockSpec((None, m, p2), lambda o: (o, 0, 0)),
        ],
        out_shape=[
            jax.ShapeDtypeStruct((m, d), BF16),
            jax.ShapeDtypeStruct((n_oct, m, p2), F32),
        ],
        compiler_params=_cparams(("parallel",)),
        name="s5_single",
    )(u, h0, bw, cw, arows, drow)


def _s5_params(a_re, a_im, log_dt, b_re, b_im, c_re, c_im, d):
    g = a_re.shape[0]
    n_oct = g // OCT_GROUPS
    ar, ai = a_re.astype(F32), a_im.astype(F32)
    dt = jnp.exp(log_dt.astype(F32))[:, None]
    mag = jnp.exp(dt * ar)
    ab_re = mag * jnp.cos(dt * ai)
    ab_im = mag * jnp.sin(dt * ai)
    den = ar * ar + ai * ai
    n_re = ab_re - 1.0
    n_im = ab_im
    f_re = (n_re * ar + n_im * ai) / den
    f_im = (n_im * ar - n_re * ai) / den
    br, bi = b_re.astype(F32), b_im.astype(F32)
    bb_re = f_re[..., None] * br - f_im[..., None] * bi
    bb_im = f_re[..., None] * bi + f_im[..., None] * br
    eye = jnp.eye(OCT_GROUPS, dtype=F32)

    def in_blocks(bb):
        t = bb.reshape(n_oct, OCT_GROUPS, STATE_DIM, GROUP_CH)
        t = jnp.einsum('ogpc,gh->ogchp', t, eye)
        return t.reshape(n_oct, LANES, OCT_STATE)

    def out_blocks(cc):
        t = cc.astype(F32).reshape(n_oct, OCT_GROUPS, GROUP_CH, STATE_DIM)
        t = jnp.einsum('ogcp,gh->ogphc', t, eye)
        return t.reshape(n_oct, OCT_STATE, LANES)

    bw = jnp.concatenate([in_blocks(bb_re), in_blocks(bb_im)], axis=2).astype(BF16)
    cw = jnp.concatenate([out_blocks(c_re), -out_blocks(c_im)], axis=1).astype(BF16)
    arows = jnp.stack([ab_re.reshape(n_oct, OCT_STATE), ab_im.reshape(n_oct, OCT_STATE)])
    drow = d.astype(F32).reshape(n_oct, 1, LANES)
    return bw, cw, arows, arows.transpose(1, 0, 2), drow


def _lambda(lp, lam_init):
    a = jnp.sum(lp[0:1] * lp[1:2], axis=1, keepdims=True)
    b = jnp.sum(lp[2:3] * lp[3:4], axis=1, keepdims=True)
    return jnp.exp(a) - jnp.exp(b) + lam_init


def _attn_kernel(qtab, ktab, pt_ref, qt_ref, k_ref, vt_ref, lp_ref, sgc_ref, sgr_ref, qcol_ref, *refs,
                 pages, groups, n_seq, n_heads, lam_init):
    k_pages, v_pages = refs[:pages], refs[pages:2 * pages]
    (kn_ref, vn_ref, rep_ref, o_ref, os_ref,
     m_scr, l_scr, a_scr, s_scr, ms_scr, ls_scr, as_scr) = refs[2 * pages:]
    step = pl.program_id(2)
    qi, ki = qtab[step], ktab[step]
    heads = k_ref.shape[1] // LANES
    n_tok = kn_ref.shape[1]
    lin = (pl.program_id(0) * pl.num_programs(1) + pl.program_id(1)) * pl.num_programs(2) + step
    active = lin < n_seq * groups
    grp = lax.rem(jnp.minimum(lin, n_seq * groups - 1), groups)

    @pl.when(ki == 0)
    def _():
        m_scr[...] = jnp.full(m_scr.shape, NEG_INF, F32)
        l_scr[...] = jnp.zeros(l_scr.shape, F32)
        a_scr[...] = jnp.zeros(a_scr.shape, F32)

    @pl.when(active & (grp == 0))
    def _():
        ms_scr[...] = jnp.full(ms_scr.shape, NEG_INF, F32)
        ls_scr[...] = jnp.zeros(ls_scr.shape, F32)
        as_scr[...] = jnp.zeros(as_scr.shape, F32)

    def sample_page(kt_ref, v_ref, n_valid):
        rows = 2 * n_heads
        rowi = lax.broadcasted_iota(jnp.int32, (rows, n_tok * n_heads), 0)
        coli = lax.broadcasted_iota(jnp.int32, (rows, n_tok * n_heads), 1)
        head_sel = (coli & (n_heads - 1)) == (rowi >> 1)
        prod = kt_ref[...] * qcol_ref[...]
        for c in range(rows):
            s_scr[c:c + 1, :] = jnp.sum(prod[c * QK_DIM:(c + 1) * QK_DIM, :], axis=0, keepdims=True)
        s = s_scr[...]
        s = jnp.where(lax.broadcasted_iota(jnp.int32, s.shape, 1) < n_valid, s, NEG_INF)
        m_old = ms_scr[...]
        m_new = jnp.maximum(m_old, jnp.max(s, axis=1, keepdims=True))
        alpha = jnp.exp2(m_old - m_new)
        p = jnp.exp2(s - m_new)
        ls_scr[...] = alpha * ls_scr[...] + jnp.sum(p, axis=1, keepdims=True)
        spread = jnp.dot(p.astype(BF16), rep_ref[...], preferred_element_type=F32)
        spread = jnp.where(head_sel, spread, 0.0).astype(BF16)
        pv = jnp.dot(spread, v_ref[...].astype(BF16), preferred_element_type=F32)
        as_scr[...] = alpha * as_scr[...] + pv
        ms_scr[...] = m_new

    def sample_pages():
        n_valid = jnp.where(active, n_tok, 0)
        for i in range(pages):
            sample_page(k_pages[i], v_pages[i], n_valid)

    def block(diagonal):
        tk, tq = k_ref.shape[0], qt_ref.shape[1]
        feat = lax.broadcasted_iota(jnp.int32, (LANES, tq), 0)
        if diagonal:
            keep = (lax.broadcasted_iota(jnp.int32, (tk, tq), 0)
                    <= lax.broadcasted_iota(jnp.int32, (tk, tq), 1))
        scores = []
        for h in range(heads):
            qt = qt_ref[h * LANES:(h + 1) * LANES, :]
            k = k_ref[:, h * LANES:(h + 1) * LANES]
            zero = jnp.zeros_like(qt)
            for qm in (jnp.where(feat < QK_DIM, qt, zero), jnp.where(feat >= QK_DIM, qt, zero)):
                scores.append(jnp.dot(k, qm, preferred_element_type=F32))
        probs, alphas = [], []
        for c, st in enumerate(scores):
            if diagonal:
                st = jnp.where(keep, st, NEG_INF)
            m_old = m_scr[c:c + 1, :]
            m_new = jnp.maximum(m_old, jnp.max(st, axis=0, keepdims=True))
            alpha = jnp.exp2(m_old - m_new)
            pt = jnp.exp2(st - m_new)
            l_scr[c:c + 1, :] = alpha * l_scr[c:c + 1, :] + jnp.sum(pt, axis=0, keepdims=True)
            m_scr[c:c + 1, :] = m_new
            probs.append(pt.astype(BF16))
            alphas.append(alpha)
        for c, (pt, alpha) in enumerate(zip(probs, alphas)):
            h = c // 2
            rows = slice(c * V_DIM, (c + 1) * V_DIM)
            vt = vt_ref[h * LANES:(h + 1) * LANES, :]
            a_scr[rows, :] = alpha * a_scr[rows, :] + jnp.dot(vt, pt, preferred_element_type=F32)

    @pl.when(ki < qi)
    def _():
        block(False)
        sample_pages()

    @pl.when(ki == qi)
    def _():
        block(True)
        sample_pages()
        lam = _lambda(lp_ref[...], lam_init)
        for h in range(heads):
            c = 2 * h
            o1 = a_scr[c * V_DIM:(c + 1) * V_DIM, :] / l_scr[c:c + 1, :]
            o2 = a_scr[(c + 1) * V_DIM:(c + 2) * V_DIM, :] / l_scr[c + 1:c + 2, :]
            ot = o1 - lam * o2
            ms = jnp.mean(ot * ot, axis=0, keepdims=True)
            ot = ot * lax.rsqrt(ms + NORM_EPS) * sgc_ref[...] * (1.0 - lam_init)
            o_ref[:, h * LANES:(h + 1) * LANES] = ot.T.astype(o_ref.dtype)

    @pl.when(active & (grp == groups - 1))
    def _():
        sample_page(kn_ref, vn_ref, 1)
        lam = _lambda(lp_ref[...], lam_init)
        s_scr[...] = as_scr[...] / ls_scr[...]
        o = s_scr[pl.ds(0, n_heads, stride=2), :] - lam * s_scr[pl.ds(1, n_heads, stride=2), :]
        os_ref[...] = (_rms(o, sgr_ref[...]) * (1.0 - lam_init)).astype(os_ref.dtype)


def _attention(qt, k, vt, qcol, kn, vn, cache_kt, cache_v2, page_table, lp, sg, *, tq, lam_init):
    b, seq, width = k.shape
    n_heads = width // LANES
    hp = FLASH_HEADS if n_heads % FLASH_HEADS == 0 else 1
    hw = hp * LANES
    nq = seq // tq
    pairs = [(i, j) for i in range(nq) for j in range(i + 1)]
    qtab = jnp.asarray([p[0] for p in pairs], jnp.int32)
    ktab = jnp.asarray([p[1] for p in pairs], jnp.int32)
    n_hg, n_pairs = n_heads // hp, len(pairs)

    n_seq, n_pages = page_table.shape
    page = cache_kt.shape[2]
    pages = PAGES_PER_STEP if n_pages % PAGES_PER_STEP == 0 else 1
    groups = n_pages // pages
    assert n_heads & (n_heads - 1) == 0 and page == LANES and n_seq * groups <= b * n_hg * n_pairs
    rep = (jnp.arange(page * n_heads)[None, :] // n_heads == jnp.arange(page)[:, None]).astype(BF16)

    def seq_grp(i, h, s):
        lin = jnp.minimum((i * n_hg + h) * n_pairs + s, n_seq * groups - 1)
        return lax.div(lin, groups), lax.rem(lin, groups)

    def per_seq(shape):
        return pl.BlockSpec((None,) + shape, lambda i, h, s, qt_, kt_, pt: (seq_grp(i, h, s)[0], 0, 0))

    def page_spec(shape, n):
        def index(i, h, s, qt_, kt_, pt):
            sq, g = seq_grp(i, h, s)
            return pt[sq, g * pages + n], 0, 0
        return pl.BlockSpec((None,) + shape, index)

    const = lambda shape: pl.BlockSpec(shape, lambda i, h, s, qt_, kt_, pt: (0, 0))
    grid_spec = pltpu.PrefetchScalarGridSpec(
        num_scalar_prefetch=3,
        grid=(b, n_hg, n_pairs),
        in_specs=[
            pl.BlockSpec((None, hw, tq), lambda i, h, s, qt_, kt_, pt: (i, h, qt_[s])),
            pl.BlockSpec((None, tq, hw), lambda i, h, s, qt_, kt_, pt: (i, kt_[s], h)),
            pl.BlockSpec((None, hw, tq), lambda i, h, s, qt_, kt_, pt: (i, h, kt_[s])),
            const((4, QK_DIM)), const((V_DIM, 1)), const((1, V_DIM)),
            per_seq((width, page))]
        + [page_spec((width, page), n) for n in range(pages)]
        + [page_spec((page * n_heads, V_DIM), n) for n in range(pages)]
        + [per_seq((width, page)), per_seq((page * n_heads, V_DIM)), const((page, page * n_heads))],
        out_specs=[
            pl.BlockSpec((None, tq, hw), lambda i, h, s, qt_, kt_, pt: (i, qt_[s], h)),
            per_seq((n_heads, V_DIM)),
        ],
        scratch_shapes=[
            pltpu.VMEM((2 * hp, tq), F32), pltpu.VMEM((2 * hp, tq), F32),
            pltpu.VMEM((2 * hp * V_DIM, tq), F32),
            pltpu.VMEM((2 * n_heads, page), F32),
            pltpu.VMEM((2 * n_heads, 1), F32), pltpu.VMEM((2 * n_heads, 1), F32),
            pltpu.VMEM((2 * n_heads, V_DIM), F32),
        ],
    )
    return pl.pallas_call(
        functools.partial(_attn_kernel, pages=pages, groups=groups, n_seq=n_seq, n_heads=n_heads,
                          lam_init=lam_init),
        grid_spec=grid_spec,
        out_shape=[jax.ShapeDtypeStruct((b, seq, width), BF16),
                   jax.ShapeDtypeStruct((n_seq, n_heads, V_DIM), BF16)],
        compiler_params=_cparams(("arbitrary", "arbitrary", "arbitrary")),
        name="diff_attention",
    )(qtab, ktab, page_table, qt, k, vt, lp, sg.reshape(V_DIM, 1), sg.reshape(1, V_DIM), qcol,
      *([cache_kt] * pages), *([cache_v2] * pages), kn, vn, rep)


def _prep_weights(ffn_w_in, ffn_w_out, s5, s5_w_glu, w_kv, attn_w_q, attn_w_o):
    assert ffn_w_out.shape[2] >= FF_TILE
    s5_prm = [_s5_params(*(t[l] for t in s5)) for l in range(s5_w_glu.shape[0])]
    return ((ffn_w_in.astype(BF16), ffn_w_out.astype(BF16)), s5_prm, s5_w_glu.astype(BF16),
            w_kv.astype(BF16), attn_w_q.astype(BF16), attn_w_o.astype(BF16))


def _trunk(x, prompt, weights, gains, h0, past_len):
    ffn_w, s5_prm, glu_w, wkv, wq, wo = weights
    norm_g, kv_norm_g, final_norm_g = gains
    depth = norm_g.shape[0]
    n_a = len(s5_prm)
    if prompt:
        b, seq, d = x.shape
        m = b * seq
        tm = min(ROW_TILE, seq // SUBLANES)
        pos = jnp.arange(seq)
    else:
        m, d = x.shape
        b, tm = 1, m
        pos = jnp.full((m,), past_len, jnp.int32)
    tn = min(COL_TILE, d)
    qk_blocks = wq.shape[2] // tn
    tables = _rope_tables(pos)
    xf = x.reshape(m, d)
    plain = lambda dt: (jax.ShapeDtypeStruct((m, d), dt), pl.BlockSpec((tm, d), lambda i, j: (i, 0)))
    proj = functools.partial(_rope_proj, tables=tables, tm=tm, tn=tn, batch=b)
    states, k_out, v_out, k_att, v_att, hn = [], None, None, None, None, None
    for l in range(depth):
        if l == n_a:
            k_outs = ((BF16, False), (F32, True)) if prompt else ((F32, False),)
            v_outs = ((F32, False), (BF16, True)) if prompt else ((F32, False),)
            k_res = proj(hn, wkv, lambda j: (0, j), wq.shape[2], rope=True, outs=k_outs)
            v_res = proj(hn, wkv, lambda j: (0, qk_blocks + j), wkv.shape[1] - wq.shape[2], rope=False, outs=v_outs)
            k_att, k_out = k_res if prompt else (None, k_res[0])
            v_out, v_att = v_res if prompt else (v_res[0], None)
        xf, hn = _ffn(xf, norm_g[l, 0], ffn_w, (l, 0), norm_g[l, 1], tm=tm, norm=plain(F32 if l < n_a else BF16))
        if l < n_a:
            if prompt:
                g, st = _s5_scan(hn.reshape(b, seq, d), h0[l], s5_prm[l], tb=tm)
                g = g.reshape(m, d)
            else:
                g, st = _s5_single(hn, h0[l], s5_prm[l])
            states.append(st)
            xf = _glu_proj(g, glu_w, l, xf, tm=tm, tn=min(GLU_COL_TILE, d))
        else:
            jl = l - n_a
            q_outs = ((BF16, True),) if prompt else ((BF16, False),)
            q = proj(hn, wq, lambda j, jl=jl: (jl, 0, j), wq.shape[2], rope=True,
                     scale=Q_SCALE * LOG2_E, outs=q_outs)[0]
            o = yield q, k_att, v_att, k_out, v_out
            xf = _res_proj(o, wo, jl, xf, tm=tm, tn=tn)
        if l == n_a - 1:
            xf, hn = _ffn(xf, norm_g[l, 2], ffn_w, (l, 1), kv_norm_g, tm=tm, norm=plain(BF16))
        elif l == depth - 1:
            (y,) = _ffn(xf, norm_g[l, 2], ffn_w, (l, 1), final_norm_g, tm=tm, emit_x=False, norm=plain(F32))
        else:
            (xf,) = _ffn(xf, norm_g[l, 2], ffn_w, (l, 1), final_norm_g, tm=tm)
    return y, states, k_out, v_out


def _advance(gen, value):
    try:
        return (next(gen) if value is None else gen.send(value)), None
    except StopIteration as stop:
        return None, stop.value


def kernel(x_prompt, x_sample, state_ssm_re, state_ssm_im, cache_k, cache_v, page_table, norm_g, ffn_w_in, ffn_w_out, s5_a_re, s5_a_im, s5_log_dt, s5_b_re, s5_b_im, s5_c_re, s5_c_im, s5_d, s5_w_glu, kv_norm_g, w_kv, attn_w_q, attn_w_o, diff_lambda, subln_g, final_norm_g):
    b, seq, d = x_prompt.shape
    n_seq = x_sample.shape[0]
    n_a, _, n_groups, p = state_ssm_re.shape
    n_oct = n_groups // OCT_GROUPS
    n_phys, page, n_heads = cache_v.shape[:3]
    assert x_sample.shape[1] == 1 and n_seq <= SAMPLE_ROWS and n_oct % SUBLANES == 0

    s5 = (s5_a_re, s5_a_im, s5_log_dt, s5_b_re, s5_b_im, s5_c_re, s5_c_im, s5_d)
    weights = _prep_weights(ffn_w_in, ffn_w_out, s5, s5_w_glu, w_kv, attn_w_q, attn_w_o)
    gains = (norm_g, kv_norm_g, final_norm_g)

    h0_p = jnp.zeros((n_a, b, n_oct // SUBLANES, SUBLANES, 2 * OCT_STATE), F32)
    prompt = _trunk(x_prompt, True, weights, gains, h0_p, None)

    pad = SAMPLE_ROWS - n_seq
    xs = jnp.pad(x_sample.reshape(n_seq, d), ((0, pad), (0, 0)))

    def block_sample(t):
        t = t.reshape(n_a, n_seq, n_oct, OCT_STATE).transpose(0, 2, 1, 3)
        return jnp.pad(t, ((0, 0), (0, 0), (0, pad), (0, 0)))

    h0_s = jnp.concatenate([block_sample(state_ssm_re), block_sample(state_ssm_im)], axis=-1)
    cache_kt = jnp.transpose(cache_k, (0, 2, 3, 4, 1)).reshape(n_phys, -1, page)
    cache_v2 = cache_v.reshape(n_phys, page * n_heads, V_DIM)
    width = cache_kt.shape[1]
    sample = _trunk(xs, False, weights, gains, h0_s, page_table.shape[1] * page)

    (req_p, res_p), (req_s, res_s) = _advance(prompt, None), _advance(sample, None)
    jl = 0
    while req_p is not None:
        lam_init = 0.8 - 0.6 * math.exp(-0.3 * (n_a + jl))
        qt, k_att, vt_att = req_p[:3]
        q_new, _, _, k_new, v_new = req_s
        qcol = jnp.broadcast_to(q_new[:n_seq].astype(F32)[:, :, None], (n_seq, width, page))
        kn = jnp.zeros((n_seq, width, page), F32).at[:, :, 0].set(k_new[:n_seq])
        vn = jnp.zeros((n_seq, page * n_heads, V_DIM), F32).at[:, :n_heads, :].set(
            v_new[:n_seq].reshape(n_seq, n_heads, V_DIM))
        o_p, o_s = _attention(qt, k_att.reshape(b, seq, width), vt_att, qcol, kn, vn, cache_kt, cache_v2,
                              page_table, diff_lambda[jl], subln_g[jl],
                              tq=min(ROW_TILE, seq // SUBLANES), lam_init=lam_init)
        o_s = jnp.pad(o_s.reshape(n_seq, width), ((0, pad), (0, 0)))
        (req_p, res_p), (req_s, res_s) = _advance(prompt, o_p.reshape(b * seq, width)), _advance(sample, o_s)
        jl += 1
    y_p, st_p, kt_p, v_p = res_p
    y_s, st_s, k_s, v_s = res_s
    k_p = jnp.transpose(kt_p.reshape(b, n_heads, 2, QK_DIM, seq), (0, 4, 1, 2, 3))

    def unblock_prompt(st, half):
        t = st[..., half * OCT_STATE:(half + 1) * OCT_STATE]
        return t.reshape(b, n_groups, p)

    def unblock_sample(st, half):
        t = st[:, :n_seq, half * OCT_STATE:(half + 1) * OCT_STATE]
        return t.transpose(1, 0, 2).reshape(n_seq, n_groups, p)

    re_p = jnp.stack([unblock_prompt(s, 0) for s in st_p])
    im_p = jnp.stack([unblock_prompt(s, 1) for s in st_p])
    re_s = jnp.stack([unblock_sample(s, 0) for s in st_s])
    im_s = jnp.stack([unblock_sample(s, 1) for s in st_s])

    return (y_p.reshape(b, seq, d), y_s[:n_seq].reshape(n_seq, 1, d), re_p, im_p,
            k_p, v_p.reshape(b, seq, n_heads, V_DIM),
            re_s, im_s,
            k_s[:n_seq].reshape(n_seq, 1, n_heads, 2, QK_DIM), v_s[:n_seq].reshape(n_seq, 1, n_heads, V_DIM))
```
